```python
import functools
import jax, jax.numpy as jnp
from jax import lax
import numpy as np

D_MODEL = 1024
BATCH = 16
SEQ = 2048
DEPTH = 4
DEC_BATCH = 8
DEC_SEQ = 64
PAST_LEN = 4096

CHUNK = 64
H_A = 8
Q_LORA = 768
KV_LORA = 256
NOPE_DIM = 64
ROPE_DIM = 32
V_DIM = 64
ROPE_BASE = 10000.0
MLA_SCALE = (NOPE_DIM + ROPE_DIM) ** -0.5
MLA_Q_BLOCK = 128
H_B = 8
D_B = 64
LEFT_CHUNKS = 8
BAND_WINDOW = LEFT_CHUNKS * CHUNK
MAX_REL = 128
BAND_SCALE = D_B ** -0.5
D_FF = 2816
ALPHA = (2 * DEPTH) ** 0.25
BETA = (8 * DEPTH) ** -0.25
NORM_EPS = 1e-5
NEG_INF = -1e30
IN_COLS = Q_LORA + KV_LORA + ROPE_DIM + 3 * H_B * D_B + 2 * D_MODEL
IN_SPLITS = (Q_LORA,
             Q_LORA + KV_LORA,
             Q_LORA + KV_LORA + ROPE_DIM,
             Q_LORA + KV_LORA + ROPE_DIM + H_B * D_B,
             Q_LORA + KV_LORA + ROPE_DIM + 2 * H_B * D_B,
             Q_LORA + KV_LORA + ROPE_DIM + 3 * H_B * D_B,
             Q_LORA + KV_LORA + ROPE_DIM + 3 * H_B * D_B + D_MODEL)

kernel_name = "mla_chunkband_macaron_deepnorm_stream"


def layer_norm(x, g, b):
    xf = x.astype(jnp.float32)
    mu = jnp.mean(xf, axis=-1, keepdims=True)
    var = jnp.mean(jnp.square(xf - mu), axis=-1, keepdims=True)
    return ((xf - mu) * lax.rsqrt(var + NORM_EPS) * g + b).astype(x.dtype)


def rms_norm(x, g):
    xf = x.astype(jnp.float32)
    return (xf * lax.rsqrt(jnp.mean(jnp.square(xf), -1, keepdims=True) + NORM_EPS) * g).astype(x.dtype)


def rope(x, pos):
    half = ROPE_DIM // 2
    inv = ROPE_BASE ** (-jnp.arange(half, dtype=jnp.float32) / half)
    ang = pos.astype(jnp.float32)[:, None] * inv[None, :]
    cos = jnp.cos(ang)[None, :, None, :]
    sin = jnp.sin(ang)[None, :, None, :]
    x1 = x[..., :half].astype(jnp.float32)
    x2 = x[..., half:].astype(jnp.float32)
    return jnp.concatenate([x1 * cos - x2 * sin, x2 * cos + x1 * sin], axis=-1).astype(x.dtype)


def swiglu(x, w1, w2):
    a, g = jnp.split(x @ w1, 2, axis=-1)
    return (jax.nn.silu(a) * g) @ w2


def rel_bias_lookup(table, rel):
    return table[:, jnp.clip(rel, -MAX_REL, MAX_REL) + MAX_REL].astype(jnp.float32)


def mla_prompt(q_abs, q_rope, ckv, kr):
    B, S = ckv.shape[:2]
    nb = S // MLA_Q_BLOCK
    key_chunk = jnp.arange(S, dtype=jnp.int32) // CHUNK

    def block(args):
        qa, qr, qpos = args
        s = (jnp.einsum('bqhc,bkc->bhqk', qa, ckv) + jnp.einsum('bqhr,bkr->bhqk', qr, kr)).astype(jnp.float32) * MLA_SCALE
        mask = key_chunk[None, :] <= (qpos // CHUNK)[:, None]
        s = jnp.where(mask[None, None], s, NEG_INF)
        p = jax.nn.softmax(s, axis=-1).astype(ckv.dtype)
        return jnp.einsum('bhqk,bkc->bqhc', p, ckv)

    to_blocks = lambda t: jnp.moveaxis(t.reshape(B, nb, MLA_Q_BLOCK, *t.shape[2:]), 1, 0)
    qpos = jnp.arange(S, dtype=jnp.int32).reshape(nb, MLA_Q_BLOCK)
    o = lax.map(block, (to_blocks(q_abs), to_blocks(q_rope), qpos))
    return jnp.moveaxis(o, 0, 1).reshape(B, S, H_A, KV_LORA)


def mla_sample(q_abs, q_rope, ckv, kr, cache_ckv, cache_kr):
    keys_c = jnp.concatenate([cache_ckv.astype(ckv.dtype), ckv], axis=1)
    keys_r = jnp.concatenate([cache_kr.astype(kr.dtype), kr], axis=1)
    s = (jnp.einsum('bqhc,bkc->bhqk', q_abs, keys_c) + jnp.einsum('bqhr,bkr->bhqk', q_rope, keys_r)).astype(jnp.float32) * MLA_SCALE
    p = jax.nn.softmax(s, axis=-1).astype(ckv.dtype)
    return jnp.einsum('bhqk,bkc->bqhc', p, keys_c)


def band_prompt(qb, kb, vb, rel_bias):
    B, S = qb.shape[:2]
    nc = S // CHUNK
    L = LEFT_CHUNKS

    def band(t):
        tp = jnp.pad(t, ((0, 0), (L * CHUNK, 0), (0, 0), (0, 0))).reshape(B, nc + L, CHUNK, H_B, D_B)
        return jnp.concatenate([tp[:, j:j + nc] for j in range(L + 1)], axis=2)

    k_band, v_band = band(kb), band(vb)
    qc = qb.reshape(B, nc, CHUNK, H_B, D_B)
    slot = jnp.arange((L + 1) * CHUNK, dtype=jnp.int32)
    rel = L * CHUNK + jnp.arange(CHUNK, dtype=jnp.int32)[:, None] - slot[None, :]
    bias = rel_bias_lookup(rel_bias, rel)
    valid = (jnp.arange(nc, dtype=jnp.int32)[:, None] - L + slot[None, :] // CHUNK) >= 0
    s = jnp.einsum('bcqhd,bckhd->bchqk', qc, k_band).astype(jnp.float32) * BAND_SCALE + bias[None, None]
    s = jnp.where(valid[None, :, None, None, :], s, NEG_INF)
    p = jax.nn.softmax(s, axis=-1).astype(qb.dtype)
    return jnp.einsum('bchqk,bckhd->bcqhd', p, v_band).reshape(B, S, H_B * D_B)


def band_sample(qb, kb, vb, rel_bias, cache_k, cache_v):
    B, T = qb.shape[:2]
    R = cache_k.shape[1]
    k = jnp.concatenate([cache_k.astype(kb.dtype), kb], axis=1)
    v = jnp.concatenate([cache_v.astype(vb.dtype), vb], axis=1)
    q_pos = R + jnp.arange(T, dtype=jnp.int32)
    k_pos = jnp.arange(R + T, dtype=jnp.int32)
    bias = rel_bias_lookup(rel_bias, q_pos[:, None] - k_pos[None, :])
    s = jnp.einsum('bqhd,bkhd->bhqk', qb, k).astype(jnp.float32) * BAND_SCALE + bias[None]
    p = jax.nn.softmax(s, axis=-1).astype(qb.dtype)
    return jnp.einsum('bhqk,bkhd->bqhd', p, v).reshape(B, T, H_B * D_B)


def trunk_layer(x, pos, mla_fn, band_fn, w):
    (ln1_g, ln1_b, f1_w1, f1_w2, w_in, q_g, w_uq, kv_g, w_uk, w_uv, rel_bias,
     w_pa, w_pb, w_out, ln2_g, ln2_b, f2_w1, f2_w2, ln3_g, ln3_b) = w
    B, S = x.shape[:2]
    x = layer_norm(ALPHA * x + 0.5 * swiglu(x, f1_w1, f1_w2), ln1_g, ln1_b)
    z = x @ w_in
    q_lat, ckv_raw, kr_raw, qb, kb, vb, g_a, g_b = jnp.split(z, IN_SPLITS, axis=-1)
    q = (rms_norm(q_lat, q_g) @ w_uq).reshape(B, S, H_A, NOPE_DIM + ROPE_DIM)
    q_rope = rope(q[..., NOPE_DIM:], pos)
    q_abs = jnp.einsum('bshn,chn->bshc', q[..., :NOPE_DIM], w_uk.reshape(KV_LORA, H_A, NOPE_DIM))
    ckv = rms_norm(ckv_raw, kv_g)
    kr = rope(kr_raw[:, :, None, :], pos)[:, :, 0, :]
    o_lat = mla_fn(q_abs, q_rope, ckv, kr)
    o_a = jnp.einsum('bshc,chv->bshv', o_lat, w_uv.reshape(KV_LORA, H_A, V_DIM)).reshape(B, S, H_A * V_DIM)
    qb = qb.reshape(B, S, H_B, D_B)
    kb = kb.reshape(B, S, H_B, D_B)
    vb = vb.reshape(B, S, H_B, D_B)
    o_b = band_fn(qb, kb, vb, rel_bias)
    mix = (jax.nn.sigmoid(g_a) * (o_a @ w_pa) + jax.nn.sigmoid(g_b) * (o_b @ w_pb)) @ w_out
    x = layer_norm(ALPHA * x + mix, ln2_g, ln2_b)
    x = layer_norm(ALPHA * x + 0.5 * swiglu(x, f2_w1, f2_w2), ln3_g, ln3_b)
    return x, ckv, kr, kb, vb


def setup_inputs(seed: int = 0) -> dict:
    key = jax.random.key(seed)
    ks = iter(jax.random.split(key, 32))

    def nrm(shape, scale):
        return scale * jax.random.normal(next(ks), shape, jnp.float32)

    def gain(shape):
        return 1.0 + nrm(shape, 0.02)

    L, D = DEPTH, D_MODEL
    band_rows = min(BAND_WINDOW, PAST_LEN)
    return {
        "x_prompt": nrm((BATCH, SEQ, D), 1.0),
        "x_sample": nrm((DEC_BATCH, DEC_SEQ, D), 1.0),
        "cache_mla_ckv": nrm((L, DEC_BATCH, PAST_LEN, KV_LORA), 1.0),
        "cache_mla_krope": nrm((L, DEC_BATCH, PAST_LEN, ROPE_DIM), 1.0),
        "cache_band_k": nrm((L, DEC_BATCH, band_rows, H_B, D_B), 1.0),
        "cache_band_v": nrm((L, DEC_BATCH, band_rows, H_B, D_B), 1.0),
        "ln1_g": gain((L, D)),
        "ln1_b": nrm((L, D), 0.02),
        "ffn1_w1": nrm((L, D, 2 * D_FF), D ** -0.5),
        "ffn1_w2": nrm((L, D_FF, D), BETA * D_FF ** -0.5),
        "w_in": nrm((L, D, IN_COLS), D ** -0.5),
        "mla_q_norm_g": gain((L, Q_LORA)),
        "mla_w_uq": nrm((L, Q_LORA, H_A * (NOPE_DIM + ROPE_DIM)), Q_LORA ** -0.5),
        "mla_kv_norm_g": gain((L, KV_LORA)),
        "mla_w_uk": nrm((L, KV_LORA, H_A * NOPE_DIM), KV_LORA ** -0.5),
        "mla_w_uv": nrm((L, KV_LORA, H_A * V_DIM), KV_LORA ** -0.5),
        "band_rel_bias": nrm((L, H_B, 2 * MAX_REL + 1), 0.5),
        "w_proj_a": nrm((L, H_A * V_DIM, D), (H_A * V_DIM) ** -0.5),
        "w_proj_b": nrm((L, H_B * D_B, D), (H_B * D_B) ** -0.5),
        "w_out": nrm((L, D, D), BETA * D ** -0.5),
        "ln2_g": gain((L, D)),
        "ln2_b": nrm((L, D), 0.02),
        "ffn2_w1": nrm((L, D, 2 * D_FF), D ** -0.5),
        "ffn2_w2": nrm((L, D_FF, D), BETA * D_FF ** -0.5),
        "ln3_g": gain((L, D)),
        "ln3_b": nrm((L, D), 0.02),
    }


def reference(x_prompt, x_sample, cache_mla_ckv, cache_mla_krope, cache_band_k, cache_band_v,
              ln1_g, ln1_b, ffn1_w1, ffn1_w2, w_in, mla_q_norm_g, mla_w_uq, mla_kv_norm_g,
              mla_w_uk, mla_w_uv, band_rel_bias, w_proj_a, w_proj_b, w_out,
              ln2_g, ln2_b, ffn2_w1, ffn2_w2, ln3_g, ln3_b):
    S = x_prompt.shape[1]
    T = x_sample.shape[1]
    past = cache_mla_ckv.shape[2]
    pos_p = jnp.arange(S, dtype=jnp.int32)
    pos_s = past + jnp.arange(T, dtype=jnp.int32)
    band_rows_p = min(BAND_WINDOW, S)
    xp, xs = x_prompt, x_sample
    ckv_p, kr_p, kb_p, vb_p = [], [], [], []
    ckv_s, kr_s, kb_s, vb_s = [], [], [], []
    for l in range(DEPTH):
        w = (ln1_g[l], ln1_b[l], ffn1_w1[l], ffn1_w2[l], w_in[l], mla_q_norm_g[l], mla_w_uq[l],
             mla_kv_norm_g[l], mla_w_uk[l], mla_w_uv[l], band_rel_bias[l], w_proj_a[l], w_proj_b[l],
             w_out[l], ln2_g[l], ln2_b[l], ffn2_w1[l], ffn2_w2[l], ln3_g[l], ln3_b[l])
        xp, c, r, k, v = trunk_layer(xp, pos_p, mla_prompt, band_prompt, w)
        ckv_p.append(c)
        kr_p.append(r)
        kb_p.append(k[:, S - band_rows_p:])
        vb_p.append(v[:, S - band_rows_p:])
        xs, c, r, k, v = trunk_layer(
            xs, pos_s,
            functools.partial(mla_sample, cache_ckv=cache_mla_ckv[l], cache_kr=cache_mla_krope[l]),
            functools.partial(band_sample, cache_k=cache_band_k[l], cache_v=cache_band_v[l]),
            w)
        ckv_s.append(c)
        kr_s.append(r)
        kb_s.append(k)
        vb_s.append(v)
    return (xp, xs,
            jnp.stack(ckv_p), jnp.stack(kr_p), jnp.stack(kb_p), jnp.stack(vb_p),
            jnp.stack(ckv_s), jnp.stack(kr_s), jnp.stack(kb_s), jnp.stack(vb_s))
```

```python
import functools

import jax
import jax.numpy as jnp
from jax import lax
from jax.experimental import pallas as pl
from jax.experimental.pallas import tpu as pltpu

D_MODEL = 1024
DEPTH = 4
CHUNK = 64
H_A = 8
Q_LORA = 768
KV_LORA = 256
NOPE_DIM = 64
ROPE_DIM = 32
V_DIM = 64
ROPE_BASE = 10000.0
MLA_SCALE = (NOPE_DIM + ROPE_DIM) ** -0.5
H_B = 8
D_B = 64
LEFT_CHUNKS = 8
MAX_REL = 128
BAND_SCALE = D_B ** -0.5
D_FF = 2816
ALPHA = (2 * DEPTH) ** 0.25
NORM_EPS = 1e-5
NEG_INF = -1e30

LANES = 128
HEAD_PAD = LANES
BAND_SLOTS = (LEFT_CHUNKS + 2) * CHUNK
BAND_FRONT = BAND_SLOTS - CHUNK
MLA_QB = 256
MLA_KB = 256
ROW_TILE = 512
VMEM_LIMIT = 56 * 1024 * 1024

BF16 = jnp.bfloat16
F32 = jnp.float32


def _dot(a, b):
    return jnp.dot(a, b, preferred_element_type=F32)


def _dot_nt(a, b):
    return lax.dot_general(a, b, (((1,), (1,)), ((), ())), preferred_element_type=F32)


def _layer_norm(y, g, b):
    mu = jnp.mean(y, axis=-1, keepdims=True)
    d = y - mu
    var = jnp.mean(d * d, axis=-1, keepdims=True)
    return d * lax.rsqrt(var + NORM_EPS) * g + b


def _rms_norm(y, g):
    return y * lax.rsqrt(jnp.mean(y * y, axis=-1, keepdims=True) + NORM_EPS) * g


def _sigmoid(a):
    return 1.0 / (1.0 + jnp.exp(-a))


def _const_spec(shape):
    nd = len(shape)
    return pl.BlockSpec(shape, lambda *_: (0,) * nd, pipeline_mode=pl.Buffered(1))


def _params(*sem):
    return pltpu.CompilerParams(dimension_semantics=sem, vmem_limit_bytes=VMEM_LIMIT)


def _ffn_ln_kernel(x_ref, w1_ref, w2_ref, g_ref, b_ref, o_ref, *, n_split):
    x = x_ref[...]
    xb = x.astype(BF16)
    ck = D_FF // n_split
    acc = None
    for c in range(n_split):
        a = _dot(xb, w1_ref[:, c * ck:(c + 1) * ck])
        g = _dot(xb, w1_ref[:, D_FF + c * ck:D_FF + (c + 1) * ck])
        h = (a * _sigmoid(a) * g).astype(BF16)
        part = _dot(h, w2_ref[c * ck:(c + 1) * ck, :])
        acc = part if acc is None else acc + part
    o_ref[...] = _layer_norm(ALPHA * x + 0.5 * acc, g_ref[...], b_ref[...])


def _ffn_ln(x, w1, w2, g, b):
    n, d = x.shape
    tm = min(ROW_TILE, n)
    row = pl.BlockSpec((tm, d), lambda i: (i, 0))
    return pl.pallas_call(
        functools.partial(_ffn_ln_kernel, n_split=2),
        grid=(n // tm,),
        in_specs=[row, _const_spec(w1.shape), _const_spec(w2.shape), _const_spec(g.shape), _const_spec(b.shape)],
        out_specs=row,
        out_shape=jax.ShapeDtypeStruct((n, d), F32),
        compiler_params=_params("parallel"),
    )(x, w1, w2, g, b)


def _proj_kernel(x_ref, cq_ref, sq_ref, ck_ref, sk_ref,
                 wq_ref, wckv_ref, wkr_ref, wkrs_ref, wband_ref, qg_ref, kvg_ref,
                 wqa_ref, wqb_ref, wkc_ref, pkr_ref, wuv_ref,
                 qcat_ref, kcat_ref, v_ref, ckv_ref, kr_ref, qb_ref, kb_ref, vb_ref):
    xb = x_ref[...].astype(BF16)
    qn = _rms_norm(_dot(xb, wq_ref[...]), qg_ref[...]).astype(BF16)
    qa = _dot(qn, wqa_ref[...])
    qs = _dot(qn, wqb_ref[...])
    cq = cq_ref[...]
    sq = sq_ref[...]
    for h in range(H_A):
        sl = slice(h * HEAD_PAD, (h + 1) * HEAD_PAD)
        qcat_ref[:, sl] = (qa[:, sl] * cq + qs[:, sl] * sq).astype(BF16)
    ckv = _rms_norm(_dot(xb, wckv_ref[...]), kvg_ref[...])
    ckv_ref[...] = ckv
    kr = _dot(xb, wkr_ref[...]) * ck_ref[...] + _dot(xb, wkrs_ref[...]) * sk_ref[...]
    kr_ref[...] = kr
    ckvb = ckv.astype(BF16)
    kcat_ref[...] = (_dot(ckvb, wkc_ref[...]) + _dot(kr.astype(BF16), pkr_ref[...])).astype(BF16)
    v_ref[...] = _dot(ckvb, wuv_ref[...]).astype(BF16)
    hb = H_B * D_B
    band = _dot(xb, wband_ref[...])
    qb_ref[...] = (band[:, :hb] * BAND_SCALE).astype(BF16)
    kb_ref[...] = band[:, hb:2 * hb]
    vb_ref[...] = band[:, 2 * hb:]


def _proj(x, tabs, w):
    n, d = x.shape
    tm = min(ROW_TILE, n)
    cq, sq, ck, sk = tabs
    period = cq.shape[0] // tm
    row = lambda c: pl.BlockSpec((tm, c), lambda i: (i, 0))
    tab = lambda c: pl.BlockSpec((tm, c), lambda i: (i % period, 0))
    hp = H_A * HEAD_PAD
    hb = H_B * D_B
    out_shape = [
        jax.ShapeDtypeStruct((n, hp), BF16),
        jax.ShapeDtypeStruct((n, hp), BF16),
        jax.ShapeDtypeStruct((n, H_A * V_DIM), BF16),
        jax.ShapeDtypeStruct((n, KV_LORA), F32),
        jax.ShapeDtypeStruct((n, ROPE_DIM), F32),
        jax.ShapeDtypeStruct((n, hb), BF16),
        jax.ShapeDtypeStruct((n, hb), F32),
        jax.ShapeDtypeStruct((n, hb), F32),
    ]
    weights = (w["w_q"], w["w_ckv"], w["w_kr"], w["w_krs"], w["w_band"], w["q_g"], w["kv_g"],
               w["w_qa"], w["w_qb"], w["w_kc"], w["p_kr"], w["w_uv"])
    return pl.pallas_call(
        _proj_kernel,
        grid=(n // tm,),
        in_specs=[row(d), tab(HEAD_PAD), tab(HEAD_PAD), tab(ROPE_DIM), tab(ROPE_DIM)]
                 + [_const_spec(a.shape) for a in weights],
        out_specs=[row(s.shape[1]) for s in out_shape],
        out_shape=out_shape,
        compiler_params=_params("parallel"),
    )(x, cq, sq, ck, sk, *weights)


def _mla_prompt_kernel(q_ref, k_ref, v_ref, o_ref):
    i = pl.program_id(1)
    qb, kb = MLA_QB, MLA_KB
    rc = lax.broadcasted_iota(jnp.int32, (qb, kb), 0) // CHUNK
    cc = lax.broadcasted_iota(jnp.int32, (qb, kb), 1) // CHUNK
    diag_mask = cc <= rc
    lane = lax.broadcasted_iota(jnp.int32, (qb, 2 * V_DIM), 1)

    def head(h):
        q = q_ref[:, h * HEAD_PAD:(h + 1) * HEAD_PAD]
        vs = slice((h // 2) * 2 * V_DIM, (h // 2 + 1) * 2 * V_DIM)

        def step(j, carry, masked):
            m, l, acc = carry
            rows = pl.ds(pl.multiple_of(j * kb, kb), kb)
            s = _dot_nt(q, k_ref[rows, h * HEAD_PAD:(h + 1) * HEAD_PAD])
            if masked:
                s = jnp.where(diag_mask, s, NEG_INF)
            m_new = jnp.maximum(m, jnp.max(s, axis=-1, keepdims=True))
            a = jnp.exp(m - m_new)
            p = jnp.exp(s - m_new)
            l = a * l + jnp.sum(p, axis=-1, keepdims=True)
            acc = a * acc + _dot(p.astype(BF16), v_ref[rows, vs])
            return m_new, l, acc

        init = (jnp.full((qb, 1), NEG_INF, F32), jnp.zeros((qb, 1), F32), jnp.zeros((qb, 2 * V_DIM), F32))
        carry = lax.fori_loop(0, i, functools.partial(step, masked=False), init)
        m, l, acc = step(i, carry, True)
        return acc / l

    for hp in range(H_A // 2):
        even = head(2 * hp)
        odd = head(2 * hp + 1)
        o_ref[:, hp * 2 * V_DIM:(hp + 1) * 2 * V_DIM] = jnp.where(lane < V_DIM, even, odd).astype(BF16)


def _mla_prompt(qcat, kcat, v, batch):
    n = qcat.shape[0]
    s = n // batch
    nq = s // MLA_QB
    return pl.pallas_call(
        _mla_prompt_kernel,
        grid=(batch, nq),
        in_specs=[pl.BlockSpec((MLA_QB, qcat.shape[1]), lambda b, i: (b * nq + i, 0)),
                  pl.BlockSpec((s, kcat.shape[1]), lambda b, i: (b, 0)),
                  pl.BlockSpec((s, v.shape[1]), lambda b, i: (b, 0))],
        out_specs=pl.BlockSpec((MLA_QB, v.shape[1]), lambda b, i: (b * nq + i, 0)),
        out_shape=jax.ShapeDtypeStruct((n, v.shape[1]), BF16),
        compiler_params=_params("parallel", "arbitrary"),
    )(qcat, kcat, v)


def _mla_sample_kernel(q_ref, ckv_ref, kr_ref, cckv_ref, ckr_ref, wabs_ref, wrope_ref, wuv_ref, o_ref):
    t = q_ref.shape[0]
    q_abs = jnp.concatenate(
        [_dot(q_ref[:, h * HEAD_PAD:(h + 1) * HEAD_PAD], wabs_ref[h]) for h in range(H_A)], axis=0).astype(BF16)
    q_rope = jnp.concatenate(
        [_dot(q_ref[:, h * HEAD_PAD:(h + 1) * HEAD_PAD], wrope_ref[h]) for h in range(H_A)], axis=0).astype(BF16)
    c_old = cckv_ref[0].astype(BF16)
    r_old = ckr_ref[0].astype(BF16)
    c_new = ckv_ref[...].astype(BF16)
    r_new = kr_ref[...].astype(BF16)
    s_old = _dot_nt(q_abs, c_old) + _dot_nt(q_rope, r_old)
    s_new = _dot_nt(q_abs, c_new) + _dot_nt(q_rope, r_new)
    m = jnp.maximum(jnp.max(s_old, axis=-1, keepdims=True), jnp.max(s_new, axis=-1, keepdims=True))
    p_old = jnp.exp(s_old - m)
    p_new = jnp.exp(s_new - m)
    l = jnp.sum(p_old, axis=-1, keepdims=True) + jnp.sum(p_new, axis=-1, keepdims=True)
    o_lat = ((_dot(p_old.astype(BF16), c_old) + _dot(p_new.astype(BF16), c_new)) / l).astype(BF16)
    group = lax.broadcasted_iota(jnp.int32, (t, H_A * V_DIM), 1) // V_DIM
    out = jnp.zeros((t, H_A * V_DIM), F32)
    for h in range(H_A):
        out = jnp.where(group == h, _dot(o_lat[h * t:(h + 1) * t], wuv_ref[...]), out)
    o_ref[...] = out.astype(BF16)


def _mla_sample(qcat, ckv, kr, cache_ckv, cache_kr, w, batch):
    n = qcat.shape[0]
    t = n // batch
    past = cache_ckv.shape[1]
    row = lambda c: pl.BlockSpec((t, c), lambda b: (b, 0))
    return pl.pallas_call(
        _mla_sample_kernel,
        grid=(batch,),
        in_specs=[row(qcat.shape[1]), row(KV_LORA), row(ROPE_DIM),
                  pl.BlockSpec((1, past, KV_LORA), lambda b: (b, 0, 0)),
                  pl.BlockSpec((1, past, ROPE_DIM), lambda b: (b, 0, 0)),
                  _const_spec(w["w_abs"].shape), _const_spec(w["w_ropesel"].shape), _const_spec(w["w_uv"].shape)],
        out_specs=row(H_A * V_DIM),
        out_shape=jax.ShapeDtypeStruct((n, H_A * V_DIM), BF16),
        compiler_params=_params("parallel"),
    )(qcat, ckv, kr, cache_ckv, cache_kr, w["w_abs"], w["w_ropesel"], w["w_uv"])


def _band_kernel(*refs, hist_rows):
    if hist_rows:
        tab_ref, q_ref, k_ref, v_ref, ck_ref, cv_ref, o_ref, kpad, vpad, bias = refs
    else:
        tab_ref, q_ref, k_ref, v_ref, o_ref, kpad, vpad, bias = refs
    s_len = q_ref.shape[0]
    hb = H_B * D_B

    @pl.when(pl.program_id(0) == 0)
    def _():
        kpad[0:BAND_FRONT - hist_rows, :] = jnp.zeros((BAND_FRONT - hist_rows, hb), BF16)
        vpad[0:BAND_FRONT - hist_rows, :] = jnp.zeros((BAND_FRONT - hist_rows, hb), BF16)
        rel = (BAND_FRONT + lax.broadcasted_iota(jnp.int32, (CHUNK, BAND_SLOTS), 0)
               - lax.broadcasted_iota(jnp.int32, (CHUNK, BAND_SLOTS), 1))
        idx = jnp.clip(rel, -MAX_REL, MAX_REL) + MAX_REL
        lo = max(BAND_FRONT - (BAND_SLOTS - 1), -MAX_REL) + MAX_REL
        for h in range(H_B):
            def fill(d, b, h=h):
                return jnp.where(idx == d, tab_ref[h, d], b)
            bias[h] = lax.fori_loop(lo, 2 * MAX_REL + 1, fill, jnp.zeros((CHUNK, BAND_SLOTS), F32))

    if hist_rows:
        kpad[BAND_FRONT - hist_rows:BAND_FRONT, :] = ck_ref[0].astype(BF16)
        vpad[BAND_FRONT - hist_rows:BAND_FRONT, :] = cv_ref[0].astype(BF16)
    kpad[BAND_FRONT:BAND_FRONT + s_len, :] = k_ref[...].astype(BF16)
    vpad[BAND_FRONT:BAND_FRONT + s_len, :] = v_ref[...].astype(BF16)

    slot = lax.broadcasted_iota(jnp.int32, (1, BAND_SLOTS), 1)
    lane = lax.broadcasted_iota(jnp.int32, (CHUNK, 2 * D_B), 1)
    first_half = lane < D_B

    def chunk(c, _):
        r0 = pl.multiple_of(c * CHUNK, CHUNK)
        q = q_ref[pl.ds(r0, CHUNK), :]
        kw = kpad[pl.ds(r0, BAND_SLOTS), :]
        vw = vpad[pl.ds(r0, BAND_SLOTS), :]
        valid = (slot >= CHUNK) & (slot >= BAND_FRONT - hist_rows - r0)
        for hp in range(H_B // 2):
            sl = slice(hp * 2 * D_B, (hp + 1) * 2 * D_B)
            qp, kp, vp = q[:, sl], kw[:, sl], vw[:, sl]
            outs = []
            for e in range(2):
                qm = jnp.where(first_half if e == 0 else ~first_half, qp, jnp.zeros_like(qp))
                s = _dot_nt(qm, kp) + bias[2 * hp + e]
                s = jnp.where(valid, s, NEG_INF)
                p = jnp.exp(s - jnp.max(s, axis=-1, keepdims=True))
                l = jnp.sum(p, axis=-1, keepdims=True)
                outs.append(_dot(p.astype(BF16), vp) / l)
            o_ref[pl.ds(r0, CHUNK), sl] = jnp.where(first_half, outs[0], outs[1]).astype(BF16)
        return 0

    lax.fori_loop(0, s_len // CHUNK, chunk, 0)


def _band(table, qb, kb, vb, batch, cache_k=None, cache_v=None):
    n, hb = qb.shape
    s = n // batch
    hist_rows = 0 if cache_k is None else cache_k.shape[1]
    row = pl.BlockSpec((s, hb), lambda b: (b, 0))
    in_specs = [pl.BlockSpec(memory_space=pltpu.SMEM), row, row, row]
    args = [table, qb, kb, vb]
    if hist_rows:
        cache = pl.BlockSpec((1, hist_rows, hb), lambda b: (b, 0, 0))
        in_specs += [cache, cache]
        args += [cache_k, cache_v]
    return pl.pallas_call(
        functools.partial(_band_kernel, hist_rows=hist_rows),
        grid=(batch,),
        in_specs=in_specs,
        out_specs=row,
        out_shape=jax.ShapeDtypeStruct((n, hb), BF16),
        scratch_shapes=[pltpu.VMEM((BAND_FRONT + s, hb), BF16), pltpu.VMEM((BAND_FRONT + s, hb), BF16),
                        pltpu.VMEM((H_B, CHUNK, BAND_SLOTS), F32)],
        compiler_params=_params("arbitrary"),
    )(*args)


def _merge_kernel(x_ref, oa_ref, ob_ref, wga_ref, wgb_ref, wpa_ref, wpb_ref, wout_ref, g_ref, b_ref, o_ref):
    x = x_ref[...]
    xb = x.astype(BF16)
    mix = (_sigmoid(_dot(xb, wga_ref[...])) * _dot(oa_ref[...], wpa_ref[...])
           + _sigmoid(_dot(xb, wgb_ref[...])) * _dot(ob_ref[...], wpb_ref[...]))
    y = ALPHA * x + _dot(mix.astype(BF16), wout_ref[...])
    o_ref[...] = _layer_norm(y, g_ref[...], b_ref[...])


def _merge(x, oa, ob, w):
    n, d = x.shape
    tm = min(ROW_TILE, n)
    row = lambda c: pl.BlockSpec((tm, c), lambda i: (i, 0))
    weights = (w["w_ga"], w["w_gb"], w["w_pa"], w["w_pb"], w["w_out"], w["ln2_g"], w["ln2_b"])
    return pl.pallas_call(
        _merge_kernel,
        grid=(n // tm,),
        in_specs=[row(d), row(oa.shape[1]), row(ob.shape[1])] + [_const_spec(a.shape) for a in weights],
        out_specs=row(d),
        out_shape=jax.ShapeDtypeStruct((n, d), F32),
        compiler_params=_params("parallel"),
    )(x, oa, ob, *weights)


def _rope_tables(pos, rows):
    half = ROPE_DIM // 2
    inv = ROPE_BASE ** (-jnp.arange(half, dtype=F32) / half)
    ang = pos.astype(F32)[:, None] * inv[None, :]
    cos, sin = jnp.cos(ang), jnp.sin(ang)
    ck = jnp.concatenate([cos, cos], axis=-1)
    sk = jnp.concatenate([-sin, sin], axis=-1)
    t = pos.shape[0]
    pad = jnp.zeros((t, HEAD_PAD - NOPE_DIM - ROPE_DIM), F32)
    cq = jnp.concatenate([jnp.ones((t, NOPE_DIM), F32), ck, pad], axis=-1) * MLA_SCALE
    sq = jnp.concatenate([jnp.zeros((t, NOPE_DIM), F32), sk, pad], axis=-1) * MLA_SCALE
    rep = rows // t
    return tuple(jnp.tile(a, (rep, 1)) for a in (cq, sq, ck, sk))


def _swap_halves(w):
    half = w.shape[-1] // 2
    return jnp.concatenate([w[..., half:], w[..., :half]], axis=-1)


def _layer_weights(l, ln1_g, ln1_b, ffn1_w1, ffn1_w2, w_in, q_g, w_uq, kv_g, w_uk, w_uv, rel_bias,
                   w_pa, w_pb, w_out, ln2_g, ln2_b, ffn2_w1, ffn2_w2, ln3_g, ln3_b):
    hb = H_B * D_B
    c0, c1, c2 = Q_LORA, Q_LORA + KV_LORA, Q_LORA + KV_LORA + ROPE_DIM
    c3 = c2 + 3 * hb
    win = w_in[l]
    w_kr = win[:, c1:c2]
    qd = NOPE_DIM + ROPE_DIM
    uq = w_uq[l].reshape(Q_LORA, H_A, qd)
    zq = jnp.zeros((Q_LORA, H_A, HEAD_PAD - qd), F32)
    w_qa = jnp.concatenate([uq, zq], axis=-1).reshape(Q_LORA, H_A * HEAD_PAD)
    w_qb = jnp.concatenate([jnp.zeros((Q_LORA, H_A, NOPE_DIM), F32), _swap_halves(uq[..., NOPE_DIM:]), zq],
                           axis=-1).reshape(Q_LORA, H_A * HEAD_PAD)
    uk = w_uk[l].reshape(KV_LORA, H_A, NOPE_DIM)
    w_kc = jnp.concatenate([uk, jnp.zeros((KV_LORA, H_A, HEAD_PAD - NOPE_DIM), F32)],
                           axis=-1).reshape(KV_LORA, H_A * HEAD_PAD)
    eye = jnp.eye(ROPE_DIM, dtype=F32)
    place = jnp.concatenate([jnp.zeros((ROPE_DIM, NOPE_DIM), F32), eye,
                             jnp.zeros((ROPE_DIM, HEAD_PAD - qd), F32)], axis=-1)
    p_kr = jnp.tile(place, (1, H_A))
    w_abs = jnp.concatenate([jnp.transpose(uk, (1, 2, 0)),
                             jnp.zeros((H_A, HEAD_PAD - NOPE_DIM, KV_LORA), F32)], axis=1)
    w_ropesel = jnp.tile(place.T[None], (H_A, 1, 1))
    bf = lambda a: a.astype(BF16)
    row = lambda a: a.reshape(1, -1)
    return dict(
        ln1_g=row(ln1_g[l]), ln1_b=row(ln1_b[l]), f1_w1=bf(ffn1_w1[l]), f1_w2=bf(ffn1_w2[l]),
        w_q=bf(win[:, :c0]), w_ckv=bf(win[:, c0:c1]), w_kr=bf(w_kr), w_krs=bf(_swap_halves(w_kr)),
        w_band=bf(win[:, c2:c3]), w_ga=bf(win[:, c3:c3 + D_MODEL]), w_gb=bf(win[:, c3 + D_MODEL:]),
        q_g=row(q_g[l]), kv_g=row(kv_g[l]), w_qa=bf(w_qa), w_qb=bf(w_qb), w_kc=bf(w_kc), p_kr=bf(p_kr),
        w_uv=bf(w_uv[l]), w_abs=bf(w_abs), w_ropesel=bf(w_ropesel), rel_bias=rel_bias[l],
        w_pa=bf(w_pa[l]), w_pb=bf(w_pb[l]), w_out=bf(w_out[l]), ln2_g=row(ln2_g[l]), ln2_b=row(ln2_b[l]),
        f2_w1=bf(ffn2_w1[l]), f2_w2=bf(ffn2_w2[l]), ln3_g=row(ln3_g[l]), ln3_b=row(ln3_b[l]),
    )


def _layer(x, tabs, w, batch, caches=None):
    x = _ffn_ln(x, w["f1_w1"], w["f1_w2"], w["ln1_g"], w["ln1_b"])
    qcat, kcat, v, ckv, kr, qb, kb, vb = _proj(x, tabs, w)
    if caches is None:
        oa = _mla_prompt(qcat, kcat, v, batch)
        ob = _band(w["rel_bias"], qb, kb, vb, batch)
    else:
        c_ckv, c_kr, c_k, c_v = caches
        oa = _mla_sample(qcat, ckv, kr, c_ckv, c_kr, w, batch)
        ob = _band(w["rel_bias"], qb, kb, vb, batch, c_k, c_v)
    x = _merge(x, oa, ob, w)
    x = _ffn_ln(x, w["f2_w1"], w["f2_w2"], w["ln3_g"], w["ln3_b"])
    return x, ckv, kr, kb, vb


def kernel(x_prompt, x_sample, cache_mla_ckv, cache_mla_krope, cache_band_k, cache_band_v, ln1_g, ln1_b, ffn1_w1, ffn1_w2, w_in, mla_q_norm_g, mla_w_uq, mla_kv_norm_g, mla_w_uk, mla_w_uv, band_rel_bias, w_proj_a, w_proj_b, w_out, ln2_g, ln2_b, ffn2_w1, ffn2_w2, ln3_g, ln3_b):
    bp, sp, d = x_prompt.shape
    bs, ts, _ = x_sample.shape
    depth = ln1_g.shape[0]
    past = cache_mla_ckv.shape[2]
    hist = cache_band_k.shape[2]
    hb = H_B * D_B
    assert d == D_MODEL and sp % MLA_QB == 0 and ts == CHUNK and hist == LEFT_CHUNKS * CHUNK
    np_, ns = bp * sp, bs * ts
    tabs_p = _rope_tables(jnp.arange(sp, dtype=jnp.int32), max(sp, min(ROW_TILE, np_)))
    tabs_s = _rope_tables(past + jnp.arange(ts, dtype=jnp.int32), min(ROW_TILE, ns))
    xp = x_prompt.reshape(np_, d)
    xs = x_sample.reshape(ns, d)
    keep = min(LEFT_CHUNKS * CHUNK, sp)
    outs_p, outs_s = [], []
    for l in range(depth):
        w = _layer_weights(l, ln1_g, ln1_b, ffn1_w1, ffn1_w2, w_in, mla_q_norm_g, mla_w_uq, mla_kv_norm_g,
                           mla_w_uk, mla_w_uv, band_rel_bias, w_proj_a, w_proj_b, w_out, ln2_g, ln2_b,
                           ffn2_w1, ffn2_w2, ln3_g, ln3_b)
        xp, ckv, kr, kb, vb = _layer(xp, tabs_p, w, bp)
        outs_p.append((ckv.reshape(bp, sp, KV_LORA), kr.reshape(bp, sp, ROPE_DIM),
                       kb.reshape(bp, sp, H_B, D_B)[:, sp - keep:], vb.reshape(bp, sp, H_B, D_B)[:, sp - keep:]))
        caches = (cache_mla_ckv[l], cache_mla_krope[l],
                  cache_band_k[l].reshape(bs, hist, hb), cache_band_v[l].reshape(bs, hist, hb))
        xs, ckv, kr, kb, vb = _layer(xs, tabs_s, w, bs, caches)
        outs_s.append((ckv.reshape(bs, ts, KV_LORA), kr.reshape(bs, ts, ROPE_DIM),
                       kb.reshape(bs, ts, H_B, D_B), vb.reshape(bs, ts, H_B, D_B)))
    stack = lambda outs, k: jnp.stack([o[k] for o in outs])
    return (xp.reshape(bp, sp, d), xs.reshape(bs, ts, d),
            stack(outs_p, 0), stack(outs_p, 1), stack(outs_p, 2), stack(outs_p, 3),
            stack(outs_s, 0), stack(outs_s, 1), stack(outs_s, 2), stack(outs_s, 3))
```

```python
import functools

import jax
import jax.numpy as jnp
from jax import lax
from jax.experimental import pallas as pl
from jax.experimental.pallas import tpu as pltpu

D_MODEL = 1024
DEPTH = 4
CHUNK = 64
H_A = 8
Q_LORA = 768
KV_LORA = 256
NOPE_DIM = 64
ROPE_DIM = 32
V_DIM = 64
ROPE_BASE = 10000.0
MLA_SCALE = (NOPE_DIM + ROPE_DIM) ** -0.5
LOG2E = 1.4426950408889634
H_B = 8
D_B = 64
LEFT_CHUNKS = 8
MAX_REL = 128
BAND_SCALE = D_B ** -0.5
D_FF = 2816
ALPHA = (2 * DEPTH) ** 0.25
NORM_EPS = 1e-5
NEG_INF = -1e30

LANES = 128
HEAD_PAD = LANES
BAND_GROUP = 4
MLA_QB = 256
ROW_TILE = 512
VMEM_LIMIT = 56 * 1024 * 1024

BF16 = jnp.bfloat16
F32 = jnp.float32


def _dot(a, b):
    return jnp.dot(a, b, preferred_element_type=F32)


def _dot_nt(a, b):
    return lax.dot_general(a, b, (((1,), (1,)), ((), ())), preferred_element_type=F32)


def _layer_norm(y, g, b):
    mu = jnp.mean(y, axis=-1, keepdims=True)
    d = y - mu
    var = jnp.mean(d * d, axis=-1, keepdims=True)
    return d * lax.rsqrt(var + NORM_EPS) * g + b


def _rms_norm(y, g):
    return y * lax.rsqrt(jnp.mean(y * y, axis=-1, keepdims=True) + NORM_EPS) * g


def _sigmoid(a):
    return 1.0 / (1.0 + jnp.exp(-a))


def _const_spec(shape):
    nd = len(shape)
    return pl.BlockSpec(shape, lambda *_: (0,) * nd, pipeline_mode=pl.Buffered(1))


def _params(*sem):
    return pltpu.CompilerParams(dimension_semantics=sem, vmem_limit_bytes=VMEM_LIMIT)


def _ffn_ln_kernel(x_ref, w1_ref, w2_ref, g_ref, b_ref, o_ref, *, n_split):
    x = x_ref[...]
    xb = x.astype(BF16)
    ck = D_FF // n_split
    acc = None
    for c in range(n_split):
        a = _dot(xb, w1_ref[:, c * ck:(c + 1) * ck])
        g = _dot(xb, w1_ref[:, D_FF + c * ck:D_FF + (c + 1) * ck])
        h = (a * _sigmoid(a) * g).astype(BF16)
        part = _dot(h, w2_ref[c * ck:(c + 1) * ck, :])
        acc = part if acc is None else acc + part
    o_ref[...] = _layer_norm(ALPHA * x + 0.5 * acc, g_ref[...], b_ref[...])


def _ffn_ln(x, w1, w2, g, b):
    n, d = x.shape
    tm = min(ROW_TILE, n)
    row = pl.BlockSpec((tm, d), lambda i: (i, 0))
    return pl.pallas_call(
        functools.partial(_ffn_ln_kernel, n_split=2),
        grid=(n // tm,),
        in_specs=[row, _const_spec(w1.shape), _const_spec(w2.shape), _const_spec(g.shape), _const_spec(b.shape)],
        out_specs=row,
        out_shape=jax.ShapeDtypeStruct((n, d), F32),
        compiler_params=_params("parallel"),
    )(x, w1, w2, g, b)


def _proj_kernel(x_ref, cq_ref, sq_ref, ck_ref, sk_ref,
                 wq_ref, wckv_ref, wkr_ref, wkrs_ref, wband_ref, qg_ref, kvg_ref,
                 wqa_ref, wqb_ref, wkc_ref, pkr_ref, wuv_ref,
                 qcat_ref, kcat_ref, v_ref, ckv_ref, kr_ref, qb_ref, kb_ref, vb_ref):
    xb = x_ref[...].astype(BF16)
    qn = _rms_norm(_dot(xb, wq_ref[...]), qg_ref[...]).astype(BF16)
    qa = _dot(qn, wqa_ref[...])
    qs = _dot(qn, wqb_ref[...])
    cq = cq_ref[...]
    sq = sq_ref[...]
    for h in range(H_A):
        sl = slice(h * HEAD_PAD, (h + 1) * HEAD_PAD)
        qcat_ref[:, sl] = (qa[:, sl] * cq + qs[:, sl] * sq).astype(BF16)
    ckv = _rms_norm(_dot(xb, wckv_ref[...]), kvg_ref[...])
    ckv_ref[...] = ckv
    kr = _dot(xb, wkr_ref[...]) * ck_ref[...] + _dot(xb, wkrs_ref[...]) * sk_ref[...]
    kr_ref[...] = kr
    ckvb = ckv.astype(BF16)
    kcat_ref[...] = (_dot(ckvb, wkc_ref[...]) + _dot(kr.astype(BF16), pkr_ref[...])).astype(BF16)
    v_ref[...] = _dot(ckvb, wuv_ref[...]).astype(BF16)
    hb = H_B * D_B
    band = _dot(xb, wband_ref[...])
    qb_ref[...] = (band[:, :hb] * (BAND_SCALE * LOG2E)).astype(BF16)
    kb_ref[...] = band[:, hb:2 * hb]
    vb_ref[...] = band[:, 2 * hb:]


def _proj(x, tabs, w):
    n, d = x.shape
    tm = min(ROW_TILE, n)
    cq, sq, ck, sk = tabs
    period = cq.shape[0] // tm
    row = lambda c: pl.BlockSpec((tm, c), lambda i: (i, 0))
    tab = lambda c: pl.BlockSpec((tm, c), lambda i: (i % period, 0))
    hp = H_A * HEAD_PAD
    hb = H_B * D_B
    out_shape = [
        jax.ShapeDtypeStruct((n, hp), BF16),
        jax.ShapeDtypeStruct((n, hp), BF16),
        jax.ShapeDtypeStruct((n, H_A * V_DIM), BF16),
        jax.ShapeDtypeStruct((n, KV_LORA), F32),
        jax.ShapeDtypeStruct((n, ROPE_DIM), F32),
        jax.ShapeDtypeStruct((n, hb), BF16),
        jax.ShapeDtypeStruct((n, hb), F32),
        jax.ShapeDtypeStruct((n, hb), F32),
    ]
    weights = (w["w_q"], w["w_ckv"], w["w_kr"], w["w_krs"], w["w_band"], w["q_g"], w["kv_g"],
               w["w_qa"], w["w_qb"], w["w_kc"], w["p_kr"], w["w_uv"])
    return pl.pallas_call(
        _proj_kernel,
        grid=(n // tm,),
        in_specs=[row(d), tab(HEAD_PAD), tab(HEAD_PAD), tab(ROPE_DIM), tab(ROPE_DIM)]
                 + [_const_spec(a.shape) for a in weights],
        out_specs=[row(s.shape[1]) for s in out_shape],
        out_shape=out_shape,
        compiler_params=_params("parallel"),
    )(x, cq, sq, ck, sk, *weights)


def _mla_prompt_kernel(q_ref, k_ref, v_ref, o_ref):
    qb = MLA_QB
    rc = lax.broadcasted_iota(jnp.int32, (qb, qb), 0) // CHUNK
    cc = lax.broadcasted_iota(jnp.int32, (qb, qb), 1) // CHUNK
    diag_mask = cc <= rc
    first_half = lax.broadcasted_iota(jnp.int32, (qb, 2 * V_DIM), 1) < V_DIM

    def head(h, kv0):
        hs = slice(h * HEAD_PAD, (h + 1) * HEAD_PAD)
        vs = slice((h // 2) * 2 * V_DIM, (h // 2 + 1) * 2 * V_DIM)
        q = q_ref[:, hs]
        s_d = jnp.where(diag_mask, _dot_nt(q, k_ref[kv0:kv0 + qb, hs]), NEG_INF)
        m = jnp.max(s_d, axis=-1, keepdims=True)
        if kv0:
            s_f = _dot_nt(q, k_ref[0:kv0, hs])
            m = jnp.maximum(m, jnp.max(s_f, axis=-1, keepdims=True))
        p_d = jnp.exp2(s_d - m)
        l = jnp.sum(p_d, axis=-1, keepdims=True)
        o = _dot(p_d.astype(BF16), v_ref[kv0:kv0 + qb, vs])
        if kv0:
            p_f = jnp.exp2(s_f - m)
            l = l + jnp.sum(p_f, axis=-1, keepdims=True)
            o = o + _dot(p_f.astype(BF16), v_ref[0:kv0, vs])
        return o / l

    def variant(n_full):
        for hp in range(H_A // 2):
            even = head(2 * hp, n_full * qb)
            odd = head(2 * hp + 1, n_full * qb)
            o_ref[:, hp * 2 * V_DIM:(hp + 1) * 2 * V_DIM] = jnp.where(first_half, even, odd).astype(BF16)

    for n in range(k_ref.shape[0] // qb):
        pl.when(pl.program_id(1) == n)(functools.partial(variant, n))


def _mla_prompt(qcat, kcat, v, batch):
    n = qcat.shape[0]
    s = n // batch
    nq = s // MLA_QB
    return pl.pallas_call(
        _mla_prompt_kernel,
        grid=(batch, nq),
        in_specs=[pl.BlockSpec((MLA_QB, qcat.shape[1]), lambda b, i: (b * nq + i, 0)),
                  pl.BlockSpec((s, kcat.shape[1]), lambda b, i: (b, 0)),
                  pl.BlockSpec((s, v.shape[1]), lambda b, i: (b, 0))],
        out_specs=pl.BlockSpec((MLA_QB, v.shape[1]), lambda b, i: (b * nq + i, 0)),
        out_shape=jax.ShapeDtypeStruct((n, v.shape[1]), BF16),
        compiler_params=_params("parallel", "arbitrary"),
    )(qcat, kcat, v)


def _mla_sample_kernel(q_ref, ckv_ref, kr_ref, cckv_ref, ckr_ref, wabs_ref, wrope_ref, wuv_ref, o_ref):
    t = q_ref.shape[0]
    q_abs = jnp.concatenate(
        [_dot(q_ref[:, h * HEAD_PAD:(h + 1) * HEAD_PAD], wabs_ref[h]) for h in range(H_A)], axis=0).astype(BF16)
    q_rope = jnp.concatenate(
        [_dot(q_ref[:, h * HEAD_PAD:(h + 1) * HEAD_PAD], wrope_ref[h]) for h in range(H_A)], axis=0).astype(BF16)
    c_old = cckv_ref[0].astype(BF16)
    r_old = ckr_ref[0].astype(BF16)
    c_new = ckv_ref[...].astype(BF16)
    r_new = kr_ref[...].astype(BF16)
    s_old = _dot_nt(q_abs, c_old) + _dot_nt(q_rope, r_old)
    s_new = _dot_nt(q_abs, c_new) + _dot_nt(q_rope, r_new)
    m = jnp.maximum(jnp.max(s_old, axis=-1, keepdims=True), jnp.max(s_new, axis=-1, keepdims=True))
    p_old = jnp.exp2(s_old - m)
    p_new = jnp.exp2(s_new - m)
    l = jnp.sum(p_old, axis=-1, keepdims=True) + jnp.sum(p_new, axis=-1, keepdims=True)
    o_lat = ((_dot(p_old.astype(BF16), c_old) + _dot(p_new.astype(BF16), c_new)) / l).astype(BF16)
    group = lax.broadcasted_iota(jnp.int32, (t, H_A * V_DIM), 1) // V_DIM
    out = jnp.zeros((t, H_A * V_DIM), F32)
    for h in range(H_A):
        out = jnp.where(group == h, _dot(o_lat[h * t:(h + 1) * t], wuv_ref[...]), out)
    o_ref[...] = out.astype(BF16)


def _mla_sample(qcat, ckv, kr, cache_ckv, cache_kr, w, batch):
    n = qcat.shape[0]
    t = n // batch
    past = cache_ckv.shape[1]
    row = lambda c: pl.BlockSpec((t, c), lambda b: (b, 0))
    return pl.pallas_call(
        _mla_sample_kernel,
        grid=(batch,),
        in_specs=[row(qcat.shape[1]), row(KV_LORA), row(ROPE_DIM),
                  pl.BlockSpec((1, past, KV_LORA), lambda b: (b, 0, 0)),
                  pl.BlockSpec((1, past, ROPE_DIM), lambda b: (b, 0, 0)),
                  _const_spec(w["w_abs"].shape), _const_spec(w["w_ropesel"].shape), _const_spec(w["w_uv"].shape)],
        out_specs=row(H_A * V_DIM),
        out_shape=jax.ShapeDtypeStruct((n, H_A * V_DIM), BF16),
        compiler_params=_params("parallel"),
    )(qcat, ckv, kr, cache_ckv, cache_kr, w["w_abs"], w["w_ropesel"], w["w_uv"])


def _band_dims(s_len):
    group = min(BAND_GROUP, s_len // CHUNK)
    window = -(-(group + LEFT_CHUNKS) * CHUNK // LANES) * LANES
    return group, window, window - group * CHUNK


def _band_kernel(*refs, hist_rows, group, window, front):
    if hist_rows:
        tab_ref, q_ref, k_ref, v_ref, ck_ref, cv_ref, o_ref, kpad, vpad, bias = refs
    else:
        tab_ref, q_ref, k_ref, v_ref, o_ref, kpad, vpad, bias = refs
    s_len = q_ref.shape[0]
    hb = H_B * D_B
    gq = group * CHUNK
    span = (LEFT_CHUNKS + 1) * CHUNK

    @pl.when(pl.program_id(0) == 0)
    def _():
        kpad[0:front - hist_rows, :] = jnp.zeros((front - hist_rows, hb), BF16)
        vpad[0:front - hist_rows, :] = jnp.zeros((front - hist_rows, hb), BF16)
        qi = lax.broadcasted_iota(jnp.int32, (CHUNK, window), 0)
        u = lax.broadcasted_iota(jnp.int32, (CHUNK, window), 1)
        idx = jnp.clip(LEFT_CHUNKS * CHUNK + qi - u, -MAX_REL, MAX_REL) + MAX_REL
        lo = max(LEFT_CHUNKS * CHUNK - (span - 1), -MAX_REL) + MAX_REL
        for h in range(H_B):
            def fill(d, b, h=h):
                return jnp.where(idx == d, tab_ref[h, d] * LOG2E, b)
            base = lax.fori_loop(lo, 2 * MAX_REL + 1, fill, jnp.zeros((CHUNK, window), F32))
            base = jnp.where(u < span, base, NEG_INF)
            for a in range(group):
                off = front + (a - LEFT_CHUNKS) * CHUNK
                bias[h, a * CHUNK:(a + 1) * CHUNK, :] = pltpu.roll(base, off, 1) if off else base

    if hist_rows:
        kpad[front - hist_rows:front, :] = ck_ref[0].astype(BF16)
        vpad[front - hist_rows:front, :] = cv_ref[0].astype(BF16)
    kpad[front:front + s_len, :] = k_ref[...].astype(BF16)
    vpad[front:front + s_len, :] = v_ref[...].astype(BF16)

    slot = lax.broadcasted_iota(jnp.int32, (1, window), 1)
    first_half = lax.broadcasted_iota(jnp.int32, (gq, 2 * D_B), 1) < D_B

    def query_group(g, _):
        r0 = pl.multiple_of(g * gq, gq)
        q = q_ref[pl.ds(r0, gq), :]
        kw = kpad[pl.ds(r0, window), :]
        vw = vpad[pl.ds(r0, window), :]
        exists = slot >= front - hist_rows - r0
        for hp in range(H_B // 2):
            sl = slice(hp * 2 * D_B, (hp + 1) * 2 * D_B)
            qp, kp, vp = q[:, sl], kw[:, sl], vw[:, sl]
            outs = []
            for e in range(2):
                qm = jnp.where(first_half if e == 0 else ~first_half, qp, jnp.zeros_like(qp))
                s = jnp.where(exists, _dot_nt(qm, kp) + bias[2 * hp + e], NEG_INF)
                p = jnp.exp2(s - jnp.max(s, axis=-1, keepdims=True))
                l = jnp.sum(p, axis=-1, keepdims=True)
                outs.append(_dot(p.astype(BF16), vp) / l)
            o_ref[pl.ds(r0, gq), sl] = jnp.where(first_half, outs[0], outs[1]).astype(BF16)
        return 0

    lax.fori_loop(0, s_len // gq, query_group, 0)


def _band(table, qb, kb, vb, batch, cache_k=None, cache_v=None):
    n, hb = qb.shape
    s = n // batch
    hist_rows = 0 if cache_k is None else cache_k.shape[1]
    group, window, front = _band_dims(s)
    row = pl.BlockSpec((s, hb), lambda b: (b, 0))
    in_specs = [pl.BlockSpec(memory_space=pltpu.SMEM), row, row, row]
    args = [table, qb, kb, vb]
    if hist_rows:
        cache = pl.BlockSpec((1, hist_rows, hb), lambda b: (b, 0, 0))
        in_specs += [cache, cache]
        args += [cache_k, cache_v]
    return pl.pallas_call(
        functools.partial(_band_kernel, hist_rows=hist_rows, group=group, window=window, front=front),
        grid=(batch,),
        in_specs=in_specs,
        out_specs=row,
        out_shape=jax.ShapeDtypeStruct((n, hb), BF16),
        scratch_shapes=[pltpu.VMEM((front + s, hb), BF16), pltpu.VMEM((front + s, hb), BF16),
                        pltpu.VMEM((H_B, group * CHUNK, window), F32)],
        compiler_params=_params("arbitrary"),
    )(*args)


def _merge_kernel(x_ref, oa_ref, ob_ref, wga_ref, wgb_ref, wpa_ref, wpb_ref, wout_ref, g_ref, b_ref, o_ref):
    x = x_ref[...]
    xb = x.astype(BF16)
    mix = (_sigmoid(_dot(xb, wga_ref[...])) * _dot(oa_ref[...], wpa_ref[...])
           + _sigmoid(_dot(xb, wgb_ref[...])) * _dot(ob_ref[...], wpb_ref[...]))
    y = ALPHA * x + _dot(mix.astype(BF16), wout_ref[...])
    o_ref[...] = _layer_norm(y, g_ref[...], b_ref[...])


def _merge(x, oa, ob, w):
    n, d = x.shape
    tm = min(ROW_TILE, n)
    row = lambda c: pl.BlockSpec((tm, c), lambda i: (i, 0))
    weights = (w["w_ga"], w["w_gb"], w["w_pa"], w["w_pb"], w["w_out"], w["ln2_g"], w["ln2_b"])
    return pl.pallas_call(
        _merge_kernel,
        grid=(n // tm,),
        in_specs=[row(d), row(oa.shape[1]), row(ob.shape[1])] + [_const_spec(a.shape) for a in weights],
        out_specs=row(d),
        out_shape=jax.ShapeDtypeStruct((n, d), F32),
        compiler_params=_params("parallel"),
    )(x, oa, ob, *weights)


def _rope_tables(pos, rows):
    half = ROPE_DIM // 2
    inv = ROPE_BASE ** (-jnp.arange(half, dtype=F32) / half)
    ang = pos.astype(F32)[:, None] * inv[None, :]
    cos, sin = jnp.cos(ang), jnp.sin(ang)
    ck = jnp.concatenate([cos, cos], axis=-1)
    sk = jnp.concatenate([-sin, sin], axis=-1)
    t = pos.shape[0]
    pad = jnp.zeros((t, HEAD_PAD - NOPE_DIM - ROPE_DIM), F32)
    cq = jnp.concatenate([jnp.ones((t, NOPE_DIM), F32), ck, pad], axis=-1) * (MLA_SCALE * LOG2E)
    sq = jnp.concatenate([jnp.zeros((t, NOPE_DIM), F32), sk, pad], axis=-1) * (MLA_SCALE * LOG2E)
    rep = rows // t
    return tuple(jnp.tile(a, (rep, 1)) for a in (cq, sq, ck, sk))


def _swap_halves(w):
    half = w.shape[-1] // 2
    return jnp.concatenate([w[..., half:], w[..., :half]], axis=-1)


def _layer_weights(l, ln1_g, ln1_b, ffn1_w1, ffn1_w2, w_in, q_g, w_uq, kv_g, w_uk, w_uv, rel_bias,
                   w_pa, w_pb, w_out, ln2_g, ln2_b, ffn2_w1, ffn2_w2, ln3_g, ln3_b):
    hb = H_B * D_B
    c0, c1, c2 = Q_LORA, Q_LORA + KV_LORA, Q_LORA + KV_LORA + ROPE_DIM
    c3 = c2 + 3 * hb
    win = w_in[l]
    w_kr = win[:, c1:c2]
    qd = NOPE_DIM + ROPE_DIM
    uq = w_uq[l].reshape(Q_LORA, H_A, qd)
    zq = jnp.zeros((Q_LORA, H_A, HEAD_PAD - qd), F32)
    w_qa = jnp.concatenate([uq, zq], axis=-1).reshape(Q_LORA, H_A * HEAD_PAD)
    w_qb = jnp.concatenate([jnp.zeros((Q_LORA, H_A, NOPE_DIM), F32), _swap_halves(uq[..., NOPE_DIM:]), zq],
                           axis=-1).reshape(Q_LORA, H_A * HEAD_PAD)
    uk = w_uk[l].reshape(KV_LORA, H_A, NOPE_DIM)
    w_kc = jnp.concatenate([uk, jnp.zeros((KV_LORA, H_A, HEAD_PAD - NOPE_DIM), F32)],
                           axis=-1).reshape(KV_LORA, H_A * HEAD_PAD)
    eye = jnp.eye(ROPE_DIM, dtype=F32)
    place = jnp.concatenate([jnp.zeros((ROPE_DIM, NOPE_DIM), F32), eye,
                             jnp.zeros((ROPE_DIM, HEAD_PAD - qd), F32)], axis=-1)
    p_kr = jnp.tile(place, (1, H_A))
    w_abs = jnp.concatenate([jnp.transpose(uk, (1, 2, 0)),
                             jnp.zeros((H_A, HEAD_PAD - NOPE_DIM, KV_LORA), F32)], axis=1)
    w_ropesel = jnp.tile(place.T[None], (H_A, 1, 1))
    bf = lambda a: a.astype(BF16)
    row = lambda a: a.reshape(1, -1)
    return dict(
        ln1_g=row(ln1_g[l]), ln1_b=row(ln1_b[l]), f1_w1=bf(ffn1_w1[l]), f1_w2=bf(ffn1_w2[l]),
        w_q=bf(win[:, :c0]), w_ckv=bf(win[:, c0:c1]), w_kr=bf(w_kr), w_krs=bf(_swap_halves(w_kr)),
        w_band=bf(win[:, c2:c3]), w_ga=bf(win[:, c3:c3 + D_MODEL]), w_gb=bf(win[:, c3 + D_MODEL:]),
        q_g=row(q_g[l]), kv_g=row(kv_g[l]), w_qa=bf(w_qa), w_qb=bf(w_qb), w_kc=bf(w_kc), p_kr=bf(p_kr),
        w_uv=bf(w_uv[l]), w_abs=bf(w_abs), w_ropesel=bf(w_ropesel), rel_bias=rel_bias[l],
        w_pa=bf(w_pa[l]), w_pb=bf(w_pb[l]), w_out=bf(w_out[l]), ln2_g=row(ln2_g[l]), ln2_b=row(ln2_b[l]),
        f2_w1=bf(ffn2_w1[l]), f2_w2=bf(ffn2_w2[l]), ln3_g=row(ln3_g[l]), ln3_b=row(ln3_b[l]),
    )


def _layer(x, tabs, w, batch, caches=None):
    x = _ffn_ln(x, w["f1_w1"], w["f1_w2"], w["ln1_g"], w["ln1_b"])
    qcat, kcat, v, ckv, kr, qb, kb, vb = _proj(x, tabs, w)
    if caches is None:
        oa = _mla_prompt(qcat, kcat, v, batch)
        ob = _band(w["rel_bias"], qb, kb, vb, batch)
    else:
        c_ckv, c_kr, c_k, c_v = caches
        oa = _mla_sample(qcat, ckv, kr, c_ckv, c_kr, w, batch)
        ob = _band(w["rel_bias"], qb, kb, vb, batch, c_k, c_v)
    x = _merge(x, oa, ob, w)
    x = _ffn_ln(x, w["f2_w1"], w["f2_w2"], w["ln3_g"], w["ln3_b"])
    return x, ckv, kr, kb, vb


def kernel(x_prompt, x_sample, cache_mla_ckv, cache_mla_krope, cache_band_k, cache_band_v, ln1_g, ln1_b, ffn1_w1, ffn1_w2, w_in, mla_q_norm_g, mla_w_uq, mla_kv_norm_g, mla_w_uk, mla_w_uv, band_rel_bias, w_proj_a, w_proj_b, w_out, ln2_g, ln2_b, ffn2_w1, ffn2_w2, ln3_g, ln3_b):
    bp, sp, d = x_prompt.shape
    bs, ts, _ = x_sample.shape
    depth = ln1_g.shape[0]
    past = cache_mla_ckv.shape[2]
    hist = cache_band_k.shape[2]
    hb = H_B * D_B
    assert d == D_MODEL and sp % MLA_QB == 0 and ts == CHUNK and hist == LEFT_CHUNKS * CHUNK
    np_, ns = bp * sp, bs * ts
    tabs_p = _rope_tables(jnp.arange(sp, dtype=jnp.int32), max(sp, min(ROW_TILE, np_)))
    tabs_s = _rope_tables(past + jnp.arange(ts, dtype=jnp.int32), min(ROW_TILE, ns))
    xp = x_prompt.reshape(np_, d)
    xs = x_sample.reshape(ns, d)
    keep = min(LEFT_CHUNKS * CHUNK, sp)
    outs_p, outs_s = [], []
    for l in range(depth):
        w = _layer_weights(l, ln1_g, ln1_b, ffn1_w1, ffn1_w2, w_in, mla_q_norm_g, mla_w_uq, mla_kv_norm_g,
                           mla_w_uk, mla_w_uv, band_rel_bias, w_proj_a, w_proj_b, w_out, ln2_g, ln2_b,
                           ffn2_w1, ffn2_w2, ln3_g, ln3_b)
        xp, ckv, kr, kb, vb = _layer(xp, tabs_p, w, bp)
        outs_p.append((ckv.reshape(bp, sp, KV_LORA), kr.reshape(bp, sp, ROPE_DIM),
                       kb.reshape(bp, sp, H_B, D_B)[:, sp - keep:], vb.reshape(bp, sp, H_B, D_B)[:, sp - keep:]))
        caches = (cache_mla_ckv[l], cache_mla_krope[l],
                  cache_band_k[l].reshape(bs, hist, hb), cache_band_v[l].reshape(bs, hist, hb))
        xs, ckv, kr, kb, vb = _layer(xs, tabs_s, w, bs, caches)
        outs_s.append((ckv.reshape(bs, ts, KV_LORA), kr.reshape(bs, ts, ROPE_DIM),
                       kb.reshape(bs, ts, H_B, D_B), vb.reshape(bs, ts, H_B, D_B)))
    stack = lambda outs, k: jnp.stack([o[k] for o in outs])
    return (xp.reshape(bp, sp, d), xs.reshape(bs, ts, d),
            stack(outs_p, 0), stack(outs_p, 1), stack(outs_p, 2), stack(outs_p, 3),
            stack(outs_s, 0), stack(outs_s, 1), stack(outs_s, 2), stack(outs_s, 3))
```

```python
import functools

import jax
import jax.numpy as jnp
from jax import lax
from jax.experimental import pallas as pl
from jax.experimental.pallas import tpu as pltpu

D_MODEL = 1024
DEPTH = 4
CHUNK = 64
H_A = 8
Q_LORA = 768
KV_LORA = 256
NOPE_DIM = 64
ROPE_DIM = 32
V_DIM = 64
ROPE_BASE = 10000.0
MLA_SCALE = (NOPE_DIM + ROPE_DIM) ** -0.5
LOG2E = 1.4426950408889634
H_B = 8
D_B = 64
LEFT_CHUNKS = 8
MAX_REL = 128
BAND_SCALE = D_B ** -0.5
D_FF = 2816
ALPHA = (2 * DEPTH) ** 0.25
NORM_EPS = 1e-5
NEG_INF = -1e30

LANES = 128
MXU_WIDTH = 256
HEAD_PAD = LANES
BAND_GROUP = 4
MLA_QB = 256
ROW_TILE = 512
VMEM_LIMIT = 56 * 1024 * 1024

BF16 = jnp.bfloat16
F32 = jnp.float32


def _dot(a, b):
    return jnp.dot(a, b, preferred_element_type=F32)


def _dot_nt(a, b):
    return lax.dot_general(a, b, (((1,), (1,)), ((), ())), preferred_element_type=F32)


def _layer_norm(y, g, b):
    mu = jnp.mean(y, axis=-1, keepdims=True)
    d = y - mu
    var = jnp.mean(d * d, axis=-1, keepdims=True)
    return d * lax.rsqrt(var + NORM_EPS) * g + b


def _rms_norm(y, g):
    return y * lax.rsqrt(jnp.mean(y * y, axis=-1, keepdims=True) + NORM_EPS) * g


def _sigmoid(a):
    return 1.0 / (1.0 + jnp.exp(-a))


def _layer_spec(a, layer):
    zeros = (0,) * (a.ndim - 1)
    return pl.BlockSpec((None,) + a.shape[1:], lambda *_: (layer,) + zeros, pipeline_mode=pl.Buffered(1))


def _params(*sem):
    return pltpu.CompilerParams(dimension_semantics=sem, vmem_limit_bytes=VMEM_LIMIT)


def _ffn_chunks():
    tiles = D_FF // MXU_WIDTH
    assert tiles * MXU_WIDTH == D_FF
    first = (tiles + 1) // 2 * MXU_WIDTH
    return ((0, first), (first, D_FF - first))


def _ffn_ln_kernel(x_ref, w1_ref, w2_ref, g_ref, b_ref, o_ref):
    x = x_ref[...]
    xb = x.astype(BF16)
    acc = None
    for c0, ck in _ffn_chunks():
        a = _dot(xb, w1_ref[:, c0:c0 + ck])
        g = _dot(xb, w1_ref[:, D_FF + c0:D_FF + c0 + ck])
        h = (a * _sigmoid(a) * g).astype(BF16)
        part = _dot(h, w2_ref[c0:c0 + ck, :])
        acc = part if acc is None else acc + part
    o_ref[...] = _layer_norm(ALPHA * x + 0.5 * acc, g_ref[...], b_ref[...])


def _ffn_ln(x, w1, w2, g, b, layer):
    n, d = x.shape
    tm = min(ROW_TILE, n)
    row = pl.BlockSpec((tm, d), lambda i: (i, 0))
    return pl.pallas_call(
        _ffn_ln_kernel,
        grid=(n // tm,),
        in_specs=[row] + [_layer_spec(a, layer) for a in (w1, w2, g, b)],
        out_specs=row,
        out_shape=jax.ShapeDtypeStruct((n, d), F32),
        compiler_params=_params("parallel"),
    )(x, w1, w2, g, b)


N_PROJ_IN = 16
N_PROJ_CACHE = 4


def _proj_kernel(*refs):
    (x_ref, cq_ref, sq_ref, ck_ref, sk_ref, wq_ref, wckv_ref, wkr_ref, wband_ref, qg_ref, kvg_ref,
     wqa_ref, wqb_ref, wkc_ref, pkr_ref, wuv_ref) = refs[:N_PROJ_IN]
    (qcat_ref, kcat_ref, v_ref, qb_ref, kb_ref, vb_ref,
     ckv_ref, kr_ref, kbt_ref, vbt_ref) = refs[len(refs) - 6 - N_PROJ_CACHE:]
    xb = x_ref[...].astype(BF16)
    qn = _rms_norm(_dot(xb, wq_ref[...]), qg_ref[...]).astype(BF16)
    qa = _dot(qn, wqa_ref[...])
    qs = _dot(qn, wqb_ref[...])
    cq = cq_ref[...]
    sq = sq_ref[...]
    for h in range(H_A):
        sl = slice(h * HEAD_PAD, (h + 1) * HEAD_PAD)
        qcat_ref[:, sl] = (qa[:, sl] * cq + qs[:, sl] * sq).astype(BF16)
    ckv = _rms_norm(_dot(xb, wckv_ref[...]), kvg_ref[...])
    ckv_ref[...] = ckv
    kr2 = _dot(xb, wkr_ref[...])
    kr = kr2[:, :ROPE_DIM] * ck_ref[...] + kr2[:, ROPE_DIM:] * sk_ref[...]
    kr_ref[...] = kr
    ckvb = ckv.astype(BF16)
    kcat_ref[...] = (_dot(ckvb, wkc_ref[...]) + _dot(kr.astype(BF16), pkr_ref[...])).astype(BF16)
    v_ref[...] = _dot(ckvb, wuv_ref[...]).astype(BF16)
    hb = H_B * D_B
    band = _dot(xb, wband_ref[...])
    qb_ref[...] = (band[:, :hb] * (BAND_SCALE * LOG2E)).astype(BF16)
    kb = band[:, hb:2 * hb]
    vb = band[:, 2 * hb:]
    kb_ref[...] = kb.astype(BF16)
    vb_ref[...] = vb.astype(BF16)
    kbt_ref[...] = kb
    vbt_ref[...] = vb


def _proj(x, tabs, w, batch, layer, depth, caches):
    n, d = x.shape
    tm = min(ROW_TILE, n)
    ntiles = n // tm
    s = n // batch
    keep = min(LEFT_CHUNKS * CHUNK, s)
    cq, sq, ck, sk = tabs
    period = cq.shape[0] // tm
    if s <= tm:
        assert keep == s
        tail_blocks, tail_idx = ntiles, (lambda i: i)
    else:
        per_seq, ntail = s // tm, keep // tm
        assert per_seq * tm == s and ntail * tm == keep
        tail_blocks = batch * ntail
        tail_idx = lambda i: (i // per_seq) * ntail + jnp.maximum(i % per_seq - (per_seq - ntail), 0)
    row = lambda c: pl.BlockSpec((tm, c), lambda i: (i, 0))
    tab = lambda c: pl.BlockSpec((tm, c), lambda i: (i % period, 0))
    stacked = lambda c: pl.BlockSpec((tm, c), lambda i: (layer * ntiles + i, 0))
    tail = lambda c: pl.BlockSpec((tm, c), lambda i: (layer * tail_blocks + tail_idx(i), 0))
    hp = H_A * HEAD_PAD
    hb = H_B * D_B
    out_shape = [
        jax.ShapeDtypeStruct((n, hp), BF16),
        jax.ShapeDtypeStruct((n, hp), BF16),
        jax.ShapeDtypeStruct((n, H_A * V_DIM), BF16),
        jax.ShapeDtypeStruct((n, hb), BF16),
        jax.ShapeDtypeStruct((n, hb), BF16),
        jax.ShapeDtypeStruct((n, hb), BF16),
        jax.ShapeDtypeStruct((depth * n, KV_LORA), F32),
        jax.ShapeDtypeStruct((depth * n, ROPE_DIM), F32),
        jax.ShapeDtypeStruct((depth * tail_blocks * tm, hb), F32),
        jax.ShapeDtypeStruct((depth * tail_blocks * tm, hb), F32),
    ]
    out_specs = [row(hp), row(hp), row(H_A * V_DIM), row(hb), row(hb), row(hb),
                 stacked(KV_LORA), stacked(ROPE_DIM), tail(hb), tail(hb)]
    weights = (w["w_q"], w["w_ckv"], w["w_kr2"], w["w_band"], w["q_g"], w["kv_g"],
               w["w_qa"], w["w_qb"], w["w_kc"], w["p_kr"], w["w_uv"])
    in_specs = ([row(d), tab(HEAD_PAD), tab(HEAD_PAD), tab(ROPE_DIM), tab(ROPE_DIM)]
                + [_layer_spec(a, layer) for a in weights])
    args = [x, cq, sq, ck, sk, *weights]
    assert len(args) == N_PROJ_IN
    aliases = {}
    if caches is not None:
        in_specs += [pl.BlockSpec(memory_space=pl.ANY)] * N_PROJ_CACHE
        args += list(caches)
        aliases = {N_PROJ_IN + k: 6 + k for k in range(N_PROJ_CACHE)}
    outs = pl.pallas_call(
        _proj_kernel,
        grid=(ntiles,),
        in_specs=in_specs,
        out_specs=out_specs,
        out_shape=out_shape,
        input_output_aliases=aliases,
        compiler_params=_params("arbitrary"),
    )(*args)
    return outs[:6], tuple(outs[6:])


def _mla_prompt_kernel(q_ref, k_ref, v_ref, o_ref):
    qb = MLA_QB
    rc = lax.broadcasted_iota(jnp.int32, (qb, qb), 0) // CHUNK
    cc = lax.broadcasted_iota(jnp.int32, (qb, qb), 1) // CHUNK
    diag_mask = cc <= rc
    first_half = lax.broadcasted_iota(jnp.int32, (qb, 2 * V_DIM), 1) < V_DIM

    def head(h, kv0):
        hs = slice(h * HEAD_PAD, (h + 1) * HEAD_PAD)
        vs = slice((h // 2) * 2 * V_DIM, (h // 2 + 1) * 2 * V_DIM)
        q = q_ref[:, hs]
        s_d = jnp.where(diag_mask, _dot_nt(q, k_ref[kv0:kv0 + qb, hs]), NEG_INF)
        m = jnp.max(s_d, axis=-1, keepdims=True)
        if kv0:
            s_f = _dot_nt(q, k_ref[0:kv0, hs])
            m = jnp.maximum(m, jnp.max(s_f, axis=-1, keepdims=True))
        p_d = jnp.exp2(s_d - m)
        l = jnp.sum(p_d, axis=-1, keepdims=True)
        o = _dot(p_d.astype(BF16), v_ref[kv0:kv0 + qb, vs])
        if kv0:
            p_f = jnp.exp2(s_f - m)
            l = l + jnp.sum(p_f, axis=-1, keepdims=True)
            o = o + _dot(p_f.astype(BF16), v_ref[0:kv0, vs])
        return o / l

    def variant(n_full):
        for hp in range(H_A // 2):
            even = head(2 * hp, n_full * qb)
            odd = head(2 * hp + 1, n_full * qb)
            o_ref[:, hp * 2 * V_DIM:(hp + 1) * 2 * V_DIM] = jnp.where(first_half, even, odd).astype(BF16)

    for n in range(k_ref.shape[0] // qb):
        pl.when(pl.program_id(1) == n)(functools.partial(variant, n))


def _mla_prompt(qcat, kcat, v, batch):
    n = qcat.shape[0]
    s = n // batch
    nq = s // MLA_QB
    return pl.pallas_call(
        _mla_prompt_kernel,
        grid=(batch, nq),
        in_specs=[pl.BlockSpec((MLA_QB, qcat.shape[1]), lambda b, i: (b * nq + i, 0)),
                  pl.BlockSpec((s, kcat.shape[1]), lambda b, i: (b, 0)),
                  pl.BlockSpec((s, v.shape[1]), lambda b, i: (b, 0))],
        out_specs=pl.BlockSpec((MLA_QB, v.shape[1]), lambda b, i: (b * nq + i, 0)),
        out_shape=jax.ShapeDtypeStruct((n, v.shape[1]), BF16),
        compiler_params=_params("parallel", "arbitrary"),
    )(qcat, kcat, v)


def _mla_sample_kernel(q_ref, ckv_ref, kr_ref, cckv_ref, ckr_ref, wabs_ref, wrope_ref, wuv_ref, o_ref):
    t = q_ref.shape[0]
    q_abs = jnp.concatenate(
        [_dot(q_ref[:, h * HEAD_PAD:(h + 1) * HEAD_PAD], wabs_ref[h]) for h in range(H_A)], axis=0).astype(BF16)
    q_rope = jnp.concatenate(
        [_dot(q_ref[:, h * HEAD_PAD:(h + 1) * HEAD_PAD], wrope_ref[h]) for h in range(H_A)], axis=0).astype(BF16)
    c_old = cckv_ref[...].astype(BF16)
    r_old = ckr_ref[...].astype(BF16)
    c_new = ckv_ref[...].astype(BF16)
    r_new = kr_ref[...].astype(BF16)
    s_old = _dot_nt(q_abs, c_old) + _dot_nt(q_rope, r_old)
    s_new = _dot_nt(q_abs, c_new) + _dot_nt(q_rope, r_new)
    m = jnp.maximum(jnp.max(s_old, axis=-1, keepdims=True), jnp.max(s_new, axis=-1, keepdims=True))
    p_old = jnp.exp2(s_old - m)
    p_new = jnp.exp2(s_new - m)
    l = jnp.sum(p_old, axis=-1, keepdims=True) + jnp.sum(p_new, axis=-1, keepdims=True)
    o_lat = ((_dot(p_old.astype(BF16), c_old) + _dot(p_new.astype(BF16), c_new)) / l).astype(BF16)
    group = lax.broadcasted_iota(jnp.int32, (t, H_A * V_DIM), 1) // V_DIM
    out = jnp.zeros((t, H_A * V_DIM), F32)
    for h in range(H_A):
        out = jnp.where(group == h, _dot(o_lat[h * t:(h + 1) * t], wuv_ref[...]), out)
    o_ref[...] = out.astype(BF16)


def _mla_sample(qcat, ckv_all, kr_all, cache_ckv, cache_kr, w, batch, layer):
    n = qcat.shape[0]
    t = n // batch
    past = cache_ckv.shape[2]
    row = lambda c: pl.BlockSpec((t, c), lambda b: (b, 0))
    new = lambda c: pl.BlockSpec((t, c), lambda b: (layer * batch + b, 0))
    old = lambda c: pl.BlockSpec((None, None, past, c), lambda b: (layer, b, 0, 0))
    weights = (w["w_abs"], w["w_ropesel"], w["w_uv"])
    return pl.pallas_call(
        _mla_sample_kernel,
        grid=(batch,),
        in_specs=[row(qcat.shape[1]), new(KV_LORA), new(ROPE_DIM), old(KV_LORA), old(ROPE_DIM)]
                 + [_layer_spec(a, layer) for a in weights],
        out_specs=row(H_A * V_DIM),
        out_shape=jax.ShapeDtypeStruct((n, H_A * V_DIM), BF16),
        compiler_params=_params("parallel"),
    )(qcat, ckv_all, kr_all, cache_ckv, cache_kr, *weights)


def _band_dims(s_len):
    group = min(BAND_GROUP, s_len // CHUNK)
    window = -(-(group + LEFT_CHUNKS) * CHUNK // LANES) * LANES
    return group, window, window - group * CHUNK


def _band_kernel(*refs, hist_rows, group, window, front):
    if hist_rows:
        tab_ref, q_ref, k_ref, v_ref, ck_ref, cv_ref, o_ref, kpad, vpad, bias = refs
    else:
        tab_ref, q_ref, k_ref, v_ref, o_ref, kpad, vpad, bias = refs
    s_len = q_ref.shape[0]
    hb = H_B * D_B
    gq = group * CHUNK
    span = (LEFT_CHUNKS + 1) * CHUNK

    @pl.when(pl.program_id(0) == 0)
    def _():
        kpad[0:front - hist_rows, :] = jnp.zeros((front - hist_rows, hb), BF16)
        vpad[0:front - hist_rows, :] = jnp.zeros((front - hist_rows, hb), BF16)
        qi = lax.broadcasted_iota(jnp.int32, (CHUNK, window), 0)
        u = lax.broadcasted_iota(jnp.int32, (CHUNK, window), 1)
        idx = jnp.clip(LEFT_CHUNKS * CHUNK + qi - u, -MAX_REL, MAX_REL) + MAX_REL
        lo = max(LEFT_CHUNKS * CHUNK - (span - 1), -MAX_REL) + MAX_REL
        for h in range(H_B):
            def fill(d, b, h=h):
                return jnp.where(idx == d, tab_ref[h, d] * LOG2E, b)
            base = lax.fori_loop(lo, 2 * MAX_REL + 1, fill, jnp.zeros((CHUNK, window), F32))
            base = jnp.where(u < span, base, NEG_INF)
            for a in range(group):
                off = front + (a - LEFT_CHUNKS) * CHUNK
                bias[h, a * CHUNK:(a + 1) * CHUNK, :] = pltpu.roll(base, off, 1) if off else base

    if hist_rows:
        kpad[front - hist_rows:front, :] = ck_ref[...].astype(BF16)
        vpad[front - hist_rows:front, :] = cv_ref[...].astype(BF16)
    kpad[front:front + s_len, :] = k_ref[...]
    vpad[front:front + s_len, :] = v_ref[...]

    slot = lax.broadcasted_iota(jnp.int32, (1, window), 1)
    first_half = lax.broadcasted_iota(jnp.int32, (gq, 2 * D_B), 1) < D_B

    def query_group(g, _):
        r0 = pl.multiple_of(g * gq, gq)
        q = q_ref[pl.ds(r0, gq), :]
        kw = kpad[pl.ds(r0, window), :]
        vw = vpad[pl.ds(r0, window), :]
        exists = slot >= front - hist_rows - r0
        for hp in range(H_B // 2):
            sl = slice(hp * 2 * D_B, (hp + 1) * 2 * D_B)
            qp, kp, vp = q[:, sl], kw[:, sl], vw[:, sl]
            outs = []
            for e in range(2):
                qm = jnp.where(first_half if e == 0 else ~first_half, qp, jnp.zeros_like(qp))
                s = jnp.where(exists, _dot_nt(qm, kp) + bias[2 * hp + e], NEG_INF)
                p = jnp.exp2(s - jnp.max(s, axis=-1, keepdims=True))
                l = jnp.sum(p, axis=-1, keepdims=True)
                outs.append(_dot(p.astype(BF16), vp) / l)
            o_ref[pl.ds(r0, gq), sl] = jnp.where(first_half, outs[0], outs[1]).astype(BF16)
        return 0

    lax.fori_loop(0, s_len // gq, query_group, 0)


def _band(table, qb, kb, vb, batch, layer, cache_k=None, cache_v=None):
    n, hb = qb.shape
    s = n // batch
    hist_rows = 0 if cache_k is None else cache_k.shape[2]
    group, window, front = _band_dims(s)
    row = pl.BlockSpec((s, hb), lambda b: (b, 0))
    in_specs = [pl.BlockSpec(memory_space=pltpu.SMEM), row, row, row]
    args = [table, qb, kb, vb]
    if hist_rows:
        cache = pl.BlockSpec((None, None, hist_rows, hb), lambda b: (layer, b, 0, 0))
        in_specs += [cache, cache]
        args += [cache_k, cache_v]
    return pl.pallas_call(
        functools.partial(_band_kernel, hist_rows=hist_rows, group=group, window=window, front=front),
        grid=(batch,),
        in_specs=in_specs,
        out_specs=row,
        out_shape=jax.ShapeDtypeStruct((n, hb), BF16),
        scratch_shapes=[pltpu.VMEM((front + s, hb), BF16), pltpu.VMEM((front + s, hb), BF16),
                        pltpu.VMEM((H_B, group * CHUNK, window), F32)],
        compiler_params=_params("arbitrary"),
    )(*args)


def _merge_kernel(x_ref, oa_ref, ob_ref, wga_ref, wgb_ref, wpa_ref, wpb_ref, wout_ref, g_ref, b_ref, o_ref):
    x = x_ref[...]
    xb = x.astype(BF16)
    mix = (_sigmoid(_dot(xb, wga_ref[...])) * _dot(oa_ref[...], wpa_ref[...])
           + _sigmoid(_dot(xb, wgb_ref[...])) * _dot(ob_ref[...], wpb_ref[...]))
    y = ALPHA * x + _dot(mix.astype(BF16), wout_ref[...])
    o_ref[...] = _layer_norm(y, g_ref[...], b_ref[...])


def _merge(x, oa, ob, w, layer):
    n, d = x.shape
    tm = min(ROW_TILE, n)
    row = lambda c: pl.BlockSpec((tm, c), lambda i: (i, 0))
    weights = (w["w_ga"], w["w_gb"], w["w_pa"], w["w_pb"], w["w_out"], w["ln2_g"], w["ln2_b"])
    return pl.pallas_call(
        _merge_kernel,
        grid=(n // tm,),
        in_specs=[row(d), row(oa.shape[1]), row(ob.shape[1])] + [_layer_spec(a, layer) for a in weights],
        out_specs=row(d),
        out_shape=jax.ShapeDtypeStruct((n, d), F32),
        compiler_params=_params("parallel"),
    )(x, oa, ob, *weights)


def _rope_tables(pos, rows):
    half = ROPE_DIM // 2
    inv = ROPE_BASE ** (-jnp.arange(half, dtype=F32) / half)
    ang = pos.astype(F32)[:, None] * inv[None, :]
    cos, sin = jnp.cos(ang), jnp.sin(ang)
    ck = jnp.concatenate([cos, cos], axis=-1)
    sk = jnp.concatenate([-sin, sin], axis=-1)
    t = pos.shape[0]
    pad = jnp.zeros((t, HEAD_PAD - NOPE_DIM - ROPE_DIM), F32)
    cq = jnp.concatenate([jnp.ones((t, NOPE_DIM), F32), ck, pad], axis=-1) * (MLA_SCALE * LOG2E)
    sq = jnp.concatenate([jnp.zeros((t, NOPE_DIM), F32), sk, pad], axis=-1) * (MLA_SCALE * LOG2E)
    rep = rows // t
    return tuple(jnp.tile(a, (rep, 1)) for a in (cq, sq, ck, sk))


def _swap_halves(w):
    half = w.shape[-1] // 2
    return jnp.concatenate([w[..., half:], w[..., :half]], axis=-1)


def _prep_weights(ln1_g, ln1_b, ffn1_w1, ffn1_w2, w_in, q_g, w_uq, kv_g, w_uk, w_uv, rel_bias,
                  w_pa, w_pb, w_out, ln2_g, ln2_b, ffn2_w1, ffn2_w2, ln3_g, ln3_b):
    depth = w_in.shape[0]
    hb = H_B * D_B
    c0, c1, c2 = Q_LORA, Q_LORA + KV_LORA, Q_LORA + KV_LORA + ROPE_DIM
    c3 = c2 + 3 * hb
    w_kr = w_in[:, :, c1:c2]
    qd = NOPE_DIM + ROPE_DIM
    uq = w_uq.reshape(depth, Q_LORA, H_A, qd)
    zq = jnp.zeros((depth, Q_LORA, H_A, HEAD_PAD - qd), F32)
    w_qa = jnp.concatenate([uq, zq], axis=-1).reshape(depth, Q_LORA, H_A * HEAD_PAD)
    w_qb = jnp.concatenate([jnp.zeros((depth, Q_LORA, H_A, NOPE_DIM), F32), _swap_halves(uq[..., NOPE_DIM:]), zq],
                           axis=-1).reshape(depth, Q_LORA, H_A * HEAD_PAD)
    uk = w_uk.reshape(depth, KV_LORA, H_A, NOPE_DIM)
    w_kc = jnp.concatenate([uk, jnp.zeros((depth, KV_LORA, H_A, HEAD_PAD - NOPE_DIM), F32)],
                           axis=-1).reshape(depth, KV_LORA, H_A * HEAD_PAD)
    eye = jnp.eye(ROPE_DIM, dtype=F32)
    place = jnp.concatenate([jnp.zeros((ROPE_DIM, NOPE_DIM), F32), eye,
                             jnp.zeros((ROPE_DIM, HEAD_PAD - qd), F32)], axis=-1)
    p_kr = jnp.broadcast_to(jnp.tile(place, (1, H_A)), (depth, ROPE_DIM, H_A * HEAD_PAD))
    w_abs = jnp.concatenate([jnp.transpose(uk, (0, 2, 3, 1)),
                             jnp.zeros((depth, H_A, HEAD_PAD - NOPE_DIM, KV_LORA), F32)], axis=2)
    w_ropesel = jnp.broadcast_to(place.T, (depth, H_A, HEAD_PAD, ROPE_DIM))
    bf = lambda a: a.astype(BF16)
    row = lambda a: a.reshape(depth, 1, -1)
    return dict(
        ln1_g=row(ln1_g), ln1_b=row(ln1_b), f1_w1=bf(ffn1_w1), f1_w2=bf(ffn1_w2),
        w_q=bf(w_in[:, :, :c0]), w_ckv=bf(w_in[:, :, c0:c1]),
        w_kr2=bf(jnp.concatenate([w_kr, _swap_halves(w_kr)], axis=-1)),
        w_band=bf(w_in[:, :, c2:c3]), w_ga=bf(w_in[:, :, c3:c3 + D_MODEL]), w_gb=bf(w_in[:, :, c3 + D_MODEL:]),
        q_g=row(q_g), kv_g=row(kv_g), w_qa=bf(w_qa), w_qb=bf(w_qb), w_kc=bf(w_kc), p_kr=bf(p_kr),
        w_uv=bf(w_uv), w_abs=bf(w_abs), w_ropesel=bf(w_ropesel), rel_bias=rel_bias,
        w_pa=bf(w_pa), w_pb=bf(w_pb), w_out=bf(w_out), ln2_g=row(ln2_g), ln2_b=row(ln2_b),
        f2_w1=bf(ffn2_w1), f2_w2=bf(ffn2_w2), ln3_g=row(ln3_g), ln3_b=row(ln3_b),
    )


def _layer(x, tabs, w, batch, layer, depth, new_caches, old_caches=None):
    x = _ffn_ln(x, w["f1_w1"], w["f1_w2"], w["ln1_g"], w["ln1_b"], layer)
    (qcat, kcat, v, qb, kb, vb), new_caches = _proj(x, tabs, w, batch, layer, depth, new_caches)
    table = w["rel_bias"][layer]
    if old_caches is None:
        oa = _mla_prompt(qcat, kcat, v, batch)
        ob = _band(table, qb, kb, vb, batch, layer)
    else:
        c_ckv, c_kr, c_k, c_v = old_caches
        oa = _mla_sample(qcat, new_caches[0], new_caches[1], c_ckv, c_kr, w, batch, layer)
        ob = _band(table, qb, kb, vb, batch, layer, c_k, c_v)
    x = _merge(x, oa, ob, w, layer)
    x = _ffn_ln(x, w["f2_w1"], w["f2_w2"], w["ln3_g"], w["ln3_b"], layer)
    return x, new_caches


def kernel(x_prompt, x_sample, cache_mla_ckv, cache_mla_krope, cache_band_k, cache_band_v, ln1_g, ln1_b, ffn1_w1, ffn1_w2, w_in, mla_q_norm_g, mla_w_uq, mla_kv_norm_g, mla_w_uk, mla_w_uv, band_rel_bias, w_proj_a, w_proj_b, w_out, ln2_g, ln2_b, ffn2_w1, ffn2_w2, ln3_g, ln3_b):
    bp, sp, d = x_prompt.shape
    bs, ts, _ = x_sample.shape
    depth = ln1_g.shape[0]
    past = cache_mla_ckv.shape[2]
    hist = cache_band_k.shape[2]
    hb = H_B * D_B
    assert d == D_MODEL and sp % MLA_QB == 0 and ts == CHUNK and hist == LEFT_CHUNKS * CHUNK
    np_, ns = bp * sp, bs * ts
    tabs_p = _rope_tables(jnp.arange(sp, dtype=jnp.int32), max(sp, min(ROW_TILE, np_)))
    tabs_s = _rope_tables(past + jnp.arange(ts, dtype=jnp.int32), min(ROW_TILE, ns))
    w = _prep_weights(ln1_g, ln1_b, ffn1_w1, ffn1_w2, w_in, mla_q_norm_g, mla_w_uq, mla_kv_norm_g,
                      mla_w_uk, mla_w_uv, band_rel_bias, w_proj_a, w_proj_b, w_out, ln2_g, ln2_b,
                      ffn2_w1, ffn2_w2, ln3_g, ln3_b)
    old = (cache_mla_ckv, cache_mla_krope,
           cache_band_k.reshape(depth, bs, hist, hb), cache_band_v.reshape(depth, bs, hist, hb))
    xp = x_prompt.reshape(np_, d)
    xs = x_sample.reshape(ns, d)
    new_p = new_s = None
    for l in range(depth):
        xp, new_p = _layer(xp, tabs_p, w, bp, l, depth, new_p)
        xs, new_s = _layer(xs, tabs_s, w, bs, l, depth, new_s, old)
    keep = min(LEFT_CHUNKS * CHUNK, sp)
    return (xp.reshape(bp, sp, d), xs.reshape(bs, ts, d),
            new_p[0].reshape(depth, bp, sp, KV_LORA), new_p[1].reshape(depth, bp, sp, ROPE_DIM),
            new_p[2].reshape(depth, bp, keep, H_B, D_B), new_p[3].reshape(depth, bp, keep, H_B, D_B),
            new_s[0].reshape(depth, bs, ts, KV_LORA), new_s[1].reshape(depth, bs, ts, ROPE_DIM),
            new_s[2].reshape(depth, bs, ts, H_B, D_B), new_s[3].reshape(depth, bs, ts, H_B, D_B))
```

```python
import functools

import jax
import jax.numpy as jnp
from jax import lax
from jax.experimental import pallas as pl
from jax.experimental.pallas import tpu as pltpu

D_MODEL = 1024
DEPTH = 4
CHUNK = 64
H_A = 8
Q_LORA = 768
KV_LORA = 256
NOPE_DIM = 64
ROPE_DIM = 32
V_DIM = 64
ROPE_BASE = 10000.0
MLA_SCALE = (NOPE_DIM + ROPE_DIM) ** -0.5
LOG2E = 1.4426950408889634
H_B = 8
D_B = 64
LEFT_CHUNKS = 8
MAX_REL = 128
BAND_SCALE = D_B ** -0.5
D_FF = 2816
ALPHA = (2 * DEPTH) ** 0.25
NORM_EPS = 1e-5
NEG_INF = -1e30

LANES = 128
MXU_WIDTH = 256
HEAD_PAD = LANES
BAND_GROUP = 4
MLA_QB = 512
ROW_TILE = 512
VMEM_LIMIT = 56 * 1024 * 1024

BF16 = jnp.bfloat16
F32 = jnp.float32


def _dot(a, b):
    return jnp.dot(a, b, preferred_element_type=F32)


def _dot_nt(a, b):
    return lax.dot_general(a, b, (((1,), (1,)), ((), ())), preferred_element_type=F32)


def _layer_norm(y, g, b):
    mu = jnp.mean(y, axis=-1, keepdims=True)
    d = y - mu
    var = jnp.mean(d * d, axis=-1, keepdims=True)
    return d * lax.rsqrt(var + NORM_EPS) * g + b


def _rms_norm(y, g):
    return y * lax.rsqrt(jnp.mean(y * y, axis=-1, keepdims=True) + NORM_EPS) * g


def _sigmoid(a):
    return 1.0 / (1.0 + jnp.exp(-a))


def _layer_spec(a, layer):
    zeros = (0,) * (a.ndim - 1)
    return pl.BlockSpec((None,) + a.shape[1:], lambda *_: (layer,) + zeros, pipeline_mode=pl.Buffered(1))


def _params(*sem):
    return pltpu.CompilerParams(dimension_semantics=sem, vmem_limit_bytes=VMEM_LIMIT)


def _ffn_chunks():
    tiles = D_FF // MXU_WIDTH
    assert tiles * MXU_WIDTH == D_FF
    first = (tiles + 1) // 2 * MXU_WIDTH
    return ((0, first), (first, D_FF - first))


def _ffn_ln_kernel(x_ref, w1_ref, w2_ref, g_ref, b_ref, o_ref):
    x = x_ref[...]
    xb = x.astype(BF16)
    acc = None
    for c0, ck in _ffn_chunks():
        a = _dot(xb, w1_ref[:, c0:c0 + ck])
        g = _dot(xb, w1_ref[:, D_FF + c0:D_FF + c0 + ck])
        h = (a * _sigmoid(a) * g).astype(BF16)
        part = _dot(h, w2_ref[c0:c0 + ck, :])
        acc = part if acc is None else acc + part
    o_ref[...] = _layer_norm(ALPHA * x + 0.5 * acc, g_ref[...], b_ref[...])


def _ffn_ln(x, w1, w2, g, b, layer):
    n, d = x.shape
    tm = min(ROW_TILE, n)
    row = pl.BlockSpec((tm, d), lambda i: (i, 0))
    return pl.pallas_call(
        _ffn_ln_kernel,
        grid=(n // tm,),
        in_specs=[row] + [_layer_spec(a, layer) for a in (w1, w2, g, b)],
        out_specs=row,
        out_shape=jax.ShapeDtypeStruct((n, d), F32),
        compiler_params=_params("parallel"),
    )(x, w1, w2, g, b)


N_PROJ_IN = 16
N_PROJ_CACHE = 4


def _proj_kernel(*refs):
    (x_ref, cq_ref, sq_ref, ck_ref, sk_ref, wq_ref, wckv_ref, wkr_ref, wband_ref, qg_ref, kvg_ref,
     wqa_ref, wqb_ref, wkc_ref, pkr_ref, wuv_ref) = refs[:N_PROJ_IN]
    (qcat_ref, kcat_ref, v_ref, qb_ref, kb_ref, vb_ref,
     ckv_ref, kr_ref, kbt_ref, vbt_ref) = refs[len(refs) - 6 - N_PROJ_CACHE:]
    xb = x_ref[...].astype(BF16)
    qn = _rms_norm(_dot(xb, wq_ref[...]), qg_ref[...]).astype(BF16)
    qa = _dot(qn, wqa_ref[...])
    qs = _dot(qn, wqb_ref[...])
    cq = cq_ref[...]
    sq = sq_ref[...]
    for h in range(H_A):
        sl = slice(h * HEAD_PAD, (h + 1) * HEAD_PAD)
        qcat_ref[:, sl] = (qa[:, sl] * cq + qs[:, sl] * sq).astype(BF16)
    ckv = _rms_norm(_dot(xb, wckv_ref[...]), kvg_ref[...])
    ckv_ref[...] = ckv
    kr2 = _dot(xb, wkr_ref[...])
    kr = kr2[:, :ROPE_DIM] * ck_ref[...] + kr2[:, ROPE_DIM:] * sk_ref[...]
    kr_ref[...] = kr
    ckvb = ckv.astype(BF16)
    kcat_ref[...] = (_dot(ckvb, wkc_ref[...]) + _dot(kr.astype(BF16), pkr_ref[...])).astype(BF16)
    v_ref[...] = _dot(ckvb, wuv_ref[...]).astype(BF16)
    hb = H_B * D_B
    band = _dot(xb, wband_ref[...])
    qb_ref[...] = (band[:, :hb] * (BAND_SCALE * LOG2E)).astype(BF16)
    kb = band[:, hb:2 * hb]
    vb = band[:, 2 * hb:]
    kb_ref[...] = kb.astype(BF16)
    vb_ref[...] = vb.astype(BF16)
    kbt_ref[...] = kb
    vbt_ref[...] = vb


def _proj(x, tabs, w, batch, layer, depth, caches):
    n, d = x.shape
    tm = min(ROW_TILE, n)
    ntiles = n // tm
    s = n // batch
    keep = min(LEFT_CHUNKS * CHUNK, s)
    cq, sq, ck, sk = tabs
    period = cq.shape[0] // tm
    if s <= tm:
        assert keep == s
        tail_blocks, tail_idx = ntiles, (lambda i: i)
    else:
        per_seq, ntail = s // tm, keep // tm
        assert per_seq * tm == s and ntail * tm == keep
        tail_blocks = batch * ntail
        tail_idx = lambda i: (i // per_seq) * ntail + jnp.maximum(i % per_seq - (per_seq - ntail), 0)
    row = lambda c: pl.BlockSpec((tm, c), lambda i: (i, 0))
    tab = lambda c: pl.BlockSpec((tm, c), lambda i: (i % period, 0))
    stacked = lambda c: pl.BlockSpec((tm, c), lambda i: (layer * ntiles + i, 0))
    tail = lambda c: pl.BlockSpec((tm, c), lambda i: (layer * tail_blocks + tail_idx(i), 0))
    hp = H_A * HEAD_PAD
    hb = H_B * D_B
    out_shape = [
        jax.ShapeDtypeStruct((n, hp), BF16),
        jax.ShapeDtypeStruct((n, hp), BF16),
        jax.ShapeDtypeStruct((n, H_A * V_DIM), BF16),
        jax.ShapeDtypeStruct((n, hb), BF16),
        jax.ShapeDtypeStruct((n, hb), BF16),
        jax.ShapeDtypeStruct((n, hb), BF16),
        jax.ShapeDtypeStruct((depth * n, KV_LORA), F32),
        jax.ShapeDtypeStruct((depth * n, ROPE_DIM), F32),
        jax.ShapeDtypeStruct((depth * tail_blocks * tm, hb), F32),
        jax.ShapeDtypeStruct((depth * tail_blocks * tm, hb), F32),
    ]
    out_specs = [row(hp), row(hp), row(H_A * V_DIM), row(hb), row(hb), row(hb),
                 stacked(KV_LORA), stacked(ROPE_DIM), tail(hb), tail(hb)]
    weights = (w["w_q"], w["w_ckv"], w["w_kr2"], w["w_band"], w["q_g"], w["kv_g"],
               w["w_qa"], w["w_qb"], w["w_kc"], w["p_kr"], w["w_uv"])
    in_specs = ([row(d), tab(HEAD_PAD), tab(HEAD_PAD), tab(ROPE_DIM), tab(ROPE_DIM)]
                + [_layer_spec(a, layer) for a in weights])
    args = [x, cq, sq, ck, sk, *weights]
    assert len(args) == N_PROJ_IN
    aliases = {}
    if caches is not None:
        in_specs += [pl.BlockSpec(memory_space=pl.ANY)] * N_PROJ_CACHE
        args += list(caches)
        aliases = {N_PROJ_IN + k: 6 + k for k in range(N_PROJ_CACHE)}
    outs = pl.pallas_call(
        _proj_kernel,
        grid=(ntiles,),
        in_specs=in_specs,
        out_specs=out_specs,
        out_shape=out_shape,
        input_output_aliases=aliases,
        compiler_params=_params("arbitrary"),
    )(*args)
    return outs[:6], tuple(outs[6:])


def _mla_prompt_kernel(q_lo_ref, q_hi_ref, k_ref, v_ref, o_ref):
    qb = MLA_QB
    rc = lax.broadcasted_iota(jnp.int32, (qb, qb), 0) // CHUNK
    cc = lax.broadcasted_iota(jnp.int32, (qb, qb), 1) // CHUNK
    diag_mask = cc <= rc
    first_half = lax.broadcasted_iota(jnp.int32, (qb, 2 * V_DIM), 1) < V_DIM

    def head(q_ref, h, kv0):
        hs = slice(h * HEAD_PAD, (h + 1) * HEAD_PAD)
        vs = slice((h // 2) * 2 * V_DIM, (h // 2 + 1) * 2 * V_DIM)
        q = q_ref[:, hs]
        s_d = jnp.where(diag_mask, _dot_nt(q, k_ref[kv0:kv0 + qb, hs]), NEG_INF)
        m = jnp.max(s_d, axis=-1, keepdims=True)
        if kv0:
            s_f = _dot_nt(q, k_ref[0:kv0, hs])
            m = jnp.maximum(m, jnp.max(s_f, axis=-1, keepdims=True))
        p_d = jnp.exp2(s_d - m)
        l = jnp.sum(p_d, axis=-1, keepdims=True)
        o = _dot(p_d.astype(BF16), v_ref[kv0:kv0 + qb, vs])
        if kv0:
            p_f = jnp.exp2(s_f - m)
            l = l + jnp.sum(p_f, axis=-1, keepdims=True)
            o = o + _dot(p_f.astype(BF16), v_ref[0:kv0, vs])
        return o / l

    def query_block(q_ref, n):
        for hp in range(H_A // 2):
            even = head(q_ref, 2 * hp, n * qb)
            odd = head(q_ref, 2 * hp + 1, n * qb)
            o_ref[n * qb:(n + 1) * qb, hp * 2 * V_DIM:(hp + 1) * 2 * V_DIM] = (
                jnp.where(first_half, even, odd).astype(BF16))

    def variant(j):
        query_block(q_lo_ref, j)
        query_block(q_hi_ref, nq - 1 - j)

    nq = k_ref.shape[0] // qb
    for j in range(nq // 2):
        pl.when(pl.program_id(1) == j)(functools.partial(variant, j))


def _mla_prompt(qcat, kcat, v, batch):
    n = qcat.shape[0]
    s = n // batch
    nq = s // MLA_QB
    assert nq % 2 == 0
    seq = lambda c: pl.BlockSpec((s, c), lambda b, j: (b, 0))
    q_lo = pl.BlockSpec((MLA_QB, qcat.shape[1]), lambda b, j: (b * nq + j, 0))
    q_hi = pl.BlockSpec((MLA_QB, qcat.shape[1]), lambda b, j: (b * nq + nq - 1 - j, 0))
    return pl.pallas_call(
        _mla_prompt_kernel,
        grid=(batch, nq // 2),
        in_specs=[q_lo, q_hi, seq(kcat.shape[1]), seq(v.shape[1])],
        out_specs=seq(v.shape[1]),
        out_shape=jax.ShapeDtypeStruct((n, v.shape[1]), BF16),
        compiler_params=_params("parallel", "arbitrary"),
    )(qcat, qcat, kcat, v)


def _mla_sample_kernel(q_ref, ckv_ref, kr_ref, cckv_ref, ckr_ref, wabs_ref, wrope_ref, wuv_ref, o_ref):
    t = q_ref.shape[0]
    q_abs = jnp.concatenate(
        [_dot(q_ref[:, h * HEAD_PAD:(h + 1) * HEAD_PAD], wabs_ref[h]) for h in range(H_A)], axis=0).astype(BF16)
    q_rope = jnp.concatenate(
        [_dot(q_ref[:, h * HEAD_PAD:(h + 1) * HEAD_PAD], wrope_ref[h]) for h in range(H_A)], axis=0).astype(BF16)
    c_old = cckv_ref[...].astype(BF16)
    r_old = ckr_ref[...].astype(BF16)
    c_new = ckv_ref[...].astype(BF16)
    r_new = kr_ref[...].astype(BF16)
    s_old = _dot_nt(q_abs, c_old) + _dot_nt(q_rope, r_old)
    s_new = _dot_nt(q_abs, c_new) + _dot_nt(q_rope, r_new)
    m = jnp.maximum(jnp.max(s_old, axis=-1, keepdims=True), jnp.max(s_new, axis=-1, keepdims=True))
    p_old = jnp.exp2(s_old - m)
    p_new = jnp.exp2(s_new - m)
    l = jnp.sum(p_old, axis=-1, keepdims=True) + jnp.sum(p_new, axis=-1, keepdims=True)
    o_lat = ((_dot(p_old.astype(BF16), c_old) + _dot(p_new.astype(BF16), c_new)) / l).astype(BF16)
    group = lax.broadcasted_iota(jnp.int32, (t, H_A * V_DIM), 1) // V_DIM
    out = jnp.zeros((t, H_A * V_DIM), F32)
    for h in range(H_A):
        out = jnp.where(group == h, _dot(o_lat[h * t:(h + 1) * t], wuv_ref[...]), out)
    o_ref[...] = out.astype(BF16)


def _mla_sample(qcat, ckv_all, kr_all, cache_ckv, cache_kr, w, batch, layer):
    n = qcat.shape[0]
    t = n // batch
    past = cache_ckv.shape[2]
    row = lambda c: pl.BlockSpec((t, c), lambda b: (b, 0))
    new = lambda c: pl.BlockSpec((t, c), lambda b: (layer * batch + b, 0))
    old = lambda c: pl.BlockSpec((None, None, past, c), lambda b: (layer, b, 0, 0))
    weights = (w["w_abs"], w["w_ropesel"], w["w_uv"])
    return pl.pallas_call(
        _mla_sample_kernel,
        grid=(batch,),
        in_specs=[row(qcat.shape[1]), new(KV_LORA), new(ROPE_DIM), old(KV_LORA), old(ROPE_DIM)]
                 + [_layer_spec(a, layer) for a in weights],
        out_specs=row(H_A * V_DIM),
        out_shape=jax.ShapeDtypeStruct((n, H_A * V_DIM), BF16),
        compiler_params=_params("parallel"),
    )(qcat, ckv_all, kr_all, cache_ckv, cache_kr, *weights)


def _band_dims(s_len):
    group = min(BAND_GROUP, s_len // CHUNK)
    window = -(-(group + LEFT_CHUNKS) * CHUNK // LANES) * LANES
    return group, window, window - group * CHUNK


def _band_kernel(*refs, hist_rows, group, window, front):
    if hist_rows:
        tab_ref, q_ref, k_ref, v_ref, ck_ref, cv_ref, o_ref, kpad, vpad, bias = refs
    else:
        tab_ref, q_ref, k_ref, v_ref, o_ref, kpad, vpad, bias = refs
    s_len = q_ref.shape[0]
    hb = H_B * D_B
    gq = group * CHUNK
    span = (LEFT_CHUNKS + 1) * CHUNK

    @pl.when(pl.program_id(0) == 0)
    def _():
        kpad[0:front - hist_rows, :] = jnp.zeros((front - hist_rows, hb), BF16)
        vpad[0:front - hist_rows, :] = jnp.zeros((front - hist_rows, hb), BF16)
        qi = lax.broadcasted_iota(jnp.int32, (CHUNK, window), 0)
        u = lax.broadcasted_iota(jnp.int32, (CHUNK, window), 1)
        idx = jnp.clip(LEFT_CHUNKS * CHUNK + qi - u, -MAX_REL, MAX_REL) + MAX_REL
        lo = max(LEFT_CHUNKS * CHUNK - (span - 1), -MAX_REL) + MAX_REL
        for h in range(H_B):
            def fill(d, b, h=h):
                return jnp.where(idx == d, tab_ref[h, d] * LOG2E, b)
            base = lax.fori_loop(lo, 2 * MAX_REL + 1, fill, jnp.zeros((CHUNK, window), F32))
            base = jnp.where(u < span, base, NEG_INF)
            for a in range(group):
                off = front + (a - LEFT_CHUNKS) * CHUNK
                r = (h % 2) * gq + a * CHUNK
                bias[h // 2, r:r + CHUNK, :] = pltpu.roll(base, off, 1) if off else base

    if hist_rows:
        kpad[front - hist_rows:front, :] = ck_ref[...].astype(BF16)
        vpad[front - hist_rows:front, :] = cv_ref[...].astype(BF16)
    kpad[front:front + s_len, :] = k_ref[...]
    vpad[front:front + s_len, :] = v_ref[...]

    slot = lax.broadcasted_iota(jnp.int32, (1, window), 1)
    first_half = lax.broadcasted_iota(jnp.int32, (gq, 2 * D_B), 1) < D_B

    def query_group(g, _, check_exists):
        r0 = pl.multiple_of(g * gq, gq)
        q = q_ref[pl.ds(r0, gq), :]
        kw = kpad[pl.ds(r0, window), :]
        vw = vpad[pl.ds(r0, window), :]
        exists = slot >= front - hist_rows - r0
        for hp in range(H_B // 2):
            sl = slice(hp * 2 * D_B, (hp + 1) * 2 * D_B)
            qp, kp, vp = q[:, sl], kw[:, sl], vw[:, sl]
            zero = jnp.zeros_like(qp)
            q2 = jnp.concatenate([jnp.where(first_half, qp, zero), jnp.where(first_half, zero, qp)], axis=0)
            s = _dot_nt(q2, kp) + bias[hp]
            if check_exists:
                s = jnp.where(exists, s, NEG_INF)
            p = jnp.exp2(s - jnp.max(s, axis=-1, keepdims=True))
            l = jnp.sum(p, axis=-1, keepdims=True)
            o = _dot(p.astype(BF16), vp) / l
            o_ref[pl.ds(r0, gq), sl] = jnp.where(first_half, o[:gq], o[gq:]).astype(BF16)
        return 0

    n_groups = s_len // gq
    n_checked = min(n_groups, -(-(front - hist_rows) // gq))
    lax.fori_loop(0, n_checked, functools.partial(query_group, check_exists=True), 0)
    lax.fori_loop(n_checked, n_groups, functools.partial(query_group, check_exists=False), 0)


def _band(table, qb, kb, vb, batch, layer, cache_k=None, cache_v=None):
    n, hb = qb.shape
    s = n // batch
    hist_rows = 0 if cache_k is None else cache_k.shape[2]
    group, window, front = _band_dims(s)
    row = pl.BlockSpec((s, hb), lambda b: (b, 0))
    in_specs = [pl.BlockSpec(memory_space=pltpu.SMEM), row, row, row]
    args = [table, qb, kb, vb]
    if hist_rows:
        cache = pl.BlockSpec((None, None, hist_rows, hb), lambda b: (layer, b, 0, 0))
        in_specs += [cache, cache]
        args += [cache_k, cache_v]
    return pl.pallas_call(
        functools.partial(_band_kernel, hist_rows=hist_rows, group=group, window=window, front=front),
        grid=(batch,),
        in_specs=in_specs,
        out_specs=row,
        out_shape=jax.ShapeDtypeStruct((n, hb), BF16),
        scratch_shapes=[pltpu.VMEM((front + s, hb), BF16), pltpu.VMEM((front + s, hb), BF16),
                        pltpu.VMEM((H_B // 2, 2 * group * CHUNK, window), F32)],
        compiler_params=_params("arbitrary"),
    )(*args)


def _merge_kernel(x_ref, oa_ref, ob_ref, wga_ref, wgb_ref, wpa_ref, wpb_ref, wout_ref, g_ref, b_ref, o_ref):
    x = x_ref[...]
    xb = x.astype(BF16)
    mix = (_sigmoid(_dot(xb, wga_ref[...])) * _dot(oa_ref[...], wpa_ref[...])
           + _sigmoid(_dot(xb, wgb_ref[...])) * _dot(ob_ref[...], wpb_ref[...]))
    y = ALPHA * x + _dot(mix.astype(BF16), wout_ref[...])
    o_ref[...] = _layer_norm(y, g_ref[...], b_ref[...])


def _merge(x, oa, ob, w, layer):
    n, d = x.shape
    tm = min(ROW_TILE, n)
    row = lambda c: pl.BlockSpec((tm, c), lambda i: (i, 0))
    weights = (w["w_ga"], w["w_gb"], w["w_pa"], w["w_pb"], w["w_out"], w["ln2_g"], w["ln2_b"])
    return pl.pallas_call(
        _merge_kernel,
        grid=(n // tm,),
        in_specs=[row(d), row(oa.shape[1]), row(ob.shape[1])] + [_layer_spec(a, layer) for a in weights],
        out_specs=row(d),
        out_shape=jax.ShapeDtypeStruct((n, d), F32),
        compiler_params=_params("parallel"),
    )(x, oa, ob, *weights)


def _rope_tables(pos, rows):
    half = ROPE_DIM // 2
    inv = ROPE_BASE ** (-jnp.arange(half, dtype=F32) / half)
    ang = pos.astype(F32)[:, None] * inv[None, :]
    cos, sin = jnp.cos(ang), jnp.sin(ang)
    ck = jnp.concatenate([cos, cos], axis=-1)
    sk = jnp.concatenate([-sin, sin], axis=-1)
    t = pos.shape[0]
    pad = jnp.zeros((t, HEAD_PAD - NOPE_DIM - ROPE_DIM), F32)
    cq = jnp.concatenate([jnp.ones((t, NOPE_DIM), F32), ck, pad], axis=-1) * (MLA_SCALE * LOG2E)
    sq = jnp.concatenate([jnp.zeros((t, NOPE_DIM), F32), sk, pad], axis=-1) * (MLA_SCALE * LOG2E)
    rep = rows // t
    return tuple(jnp.tile(a, (rep, 1)) for a in (cq, sq, ck, sk))


def _swap_halves(w):
    half = w.shape[-1] // 2
    return jnp.concatenate([w[..., half:], w[..., :half]], axis=-1)


def _prep_weights(ln1_g, ln1_b, ffn1_w1, ffn1_w2, w_in, q_g, w_uq, kv_g, w_uk, w_uv, rel_bias,
                  w_pa, w_pb, w_out, ln2_g, ln2_b, ffn2_w1, ffn2_w2, ln3_g, ln3_b):
    depth = w_in.shape[0]
    hb = H_B * D_B
    c0, c1, c2 = Q_LORA, Q_LORA + KV_LORA, Q_LORA + KV_LORA + ROPE_DIM
    c3 = c2 + 3 * hb
    w_kr = w_in[:, :, c1:c2]
    qd = NOPE_DIM + ROPE_DIM
    uq = w_uq.reshape(depth, Q_LORA, H_A, qd)
    zq = jnp.zeros((depth, Q_LORA, H_A, HEAD_PAD - qd), F32)
    w_qa = jnp.concatenate([uq, zq], axis=-1).reshape(depth, Q_LORA, H_A * HEAD_PAD)
    w_qb = jnp.concatenate([jnp.zeros((depth, Q_LORA, H_A, NOPE_DIM), F32), _swap_halves(uq[..., NOPE_DIM:]), zq],
                           axis=-1).reshape(depth, Q_LORA, H_A * HEAD_PAD)
    uk = w_uk.reshape(depth, KV_LORA, H_A, NOPE_DIM)
    w_kc = jnp.concatenate([uk, jnp.zeros((depth, KV_LORA, H_A, HEAD_PAD - NOPE_DIM), F32)],
                           axis=-1).reshape(depth, KV_LORA, H_A * HEAD_PAD)
    eye = jnp.eye(ROPE_DIM, dtype=F32)
    place = jnp.concatenate([jnp.zeros((ROPE_DIM, NOPE_DIM), F32), eye,
                             jnp.zeros((ROPE_DIM, HEAD_PAD - qd), F32)], axis=-1)
    p_kr = jnp.broadcast_to(jnp.tile(place, (1, H_A)), (depth, ROPE_DIM, H_A * HEAD_PAD))
    w_abs = jnp.concatenate([jnp.transpose(uk, (0, 2, 3, 1)),
                             jnp.zeros((depth, H_A, HEAD_PAD - NOPE_DIM, KV_LORA), F32)], axis=2)
    w_ropesel = jnp.broadcast_to(place.T, (depth, H_A, HEAD_PAD, ROPE_DIM))
    bf = lambda a: a.astype(BF16)
    row = lambda a: a.reshape(depth, 1, -1)
    return dict(
        ln1_g=row(ln1_g), ln1_b=row(ln1_b), f1_w1=bf(ffn1_w1), f1_w2=bf(ffn1_w2),
        w_q=bf(w_in[:, :, :c0]), w_ckv=bf(w_in[:, :, c0:c1]),
        w_kr2=bf(jnp.concatenate([w_kr, _swap_halves(w_kr)], axis=-1)),
        w_band=bf(w_in[:, :, c2:c3]), w_ga=bf(w_in[:, :, c3:c3 + D_MODEL]), w_gb=bf(w_in[:, :, c3 + D_MODEL:]),
        q_g=row(q_g), kv_g=row(kv_g), w_qa=bf(w_qa), w_qb=bf(w_qb), w_kc=bf(w_kc), p_kr=bf(p_kr),
        w_uv=bf(w_uv), w_abs=bf(w_abs), w_ropesel=bf(w_ropesel), rel_bias=rel_bias,
        w_pa=bf(w_pa), w_pb=bf(w_pb), w_out=bf(w_out), ln2_g=row(ln2_g), ln2_b=row(ln2_b),
        f2_w1=bf(ffn2_w1), f2_w2=bf(ffn2_w2), ln3_g=row(ln3_g), ln3_b=row(ln3_b),
    )


def _layer(x, tabs, w, batch, layer, depth, new_caches, old_caches=None):
    x = _ffn_ln(x, w["f1_w1"], w["f1_w2"], w["ln1_g"], w["ln1_b"], layer)
    (qcat, kcat, v, qb, kb, vb), new_caches = _proj(x, tabs, w, batch, layer, depth, new_caches)
    table = w["rel_bias"][layer]
    if old_caches is None:
        oa = _mla_prompt(qcat, kcat, v, batch)
        ob = _band(table, qb, kb, vb, batch, layer)
    else:
        c_ckv, c_kr, c_k, c_v = old_caches
        oa = _mla_sample(qcat, new_caches[0], new_caches[1], c_ckv, c_kr, w, batch, layer)
        ob = _band(table, qb, kb, vb, batch, layer, c_k, c_v)
    x = _merge(x, oa, ob, w, layer)
    x = _ffn_ln(x, w["f2_w1"], w["f2_w2"], w["ln3_g"], w["ln3_b"], layer)
    return x, new_caches


def kernel(x_prompt, x_sample, cache_mla_ckv, cache_mla_krope, cache_band_k, cache_band_v, ln1_g, ln1_b, ffn1_w1, ffn1_w2, w_in, mla_q_norm_g, mla_w_uq, mla_kv_norm_g, mla_w_uk, mla_w_uv, band_rel_bias, w_proj_a, w_proj_b, w_out, ln2_g, ln2_b, ffn2_w1, ffn2_w2, ln3_g, ln3_b):
    bp, sp, d = x_prompt.shape
    bs, ts, _ = x_sample.shape
    depth = ln1_g.shape[0]
    past = cache_mla_ckv.shape[2]
    hist = cache_band_k.shape[2]
    hb = H_B * D_B
    assert d == D_MODEL and sp % MLA_QB == 0 and ts == CHUNK and hist == LEFT_CHUNKS * CHUNK
    np_, ns = bp * sp, bs * ts
    tabs_p = _rope_tables(jnp.arange(sp, dtype=jnp.int32), max(sp, min(ROW_TILE, np_)))
    tabs_s = _rope_tables(past + jnp.arange(ts, dtype=jnp.int32), min(ROW_TILE, ns))
    w = _prep_weights(ln1_g, ln1_b, ffn1_w1, ffn1_w2, w_in, mla_q_norm_g, mla_w_uq, mla_kv_norm_g,
                      mla_w_uk, mla_w_uv, band_rel_bias, w_proj_a, w_proj_b, w_out, ln2_g, ln2_b,
                      ffn2_w1, ffn2_w2, ln3_g, ln3_b)
    old = (cache_mla_ckv, cache_mla_krope,
           cache_band_k.reshape(depth, bs, hist, hb), cache_band_v.reshape(depth, bs, hist, hb))
    xp = x_prompt.reshape(np_, d)
    xs = x_sample.reshape(ns, d)
    new_p = new_s = None
    for l in range(depth):
        xp, new_p = _layer(xp, tabs_p, w, bp, l, depth, new_p)
        xs, new_s = _layer(xs, tabs_s, w, bs, l, depth, new_s, old)
    keep = min(LEFT_CHUNKS * CHUNK, sp)
    return (xp.reshape(bp, sp, d), xs.reshape(bs, ts, d),
            new_p[0].reshape(depth, bp, sp, KV_LORA), new_p[1].reshape(depth, bp, sp, ROPE_DIM),
            new_p[2].reshape(depth, bp, keep, H_B, D_B), new_p[3].reshape(depth, bp, keep, H_B, D_B),
            new_s[0].reshape(depth, bs, ts, KV_LORA), new_s[1].reshape(depth, bs, ts, ROPE_DIM),
            new_s[2].reshape(depth, bs, ts, H_B, D_B), new_s[3].reshape(depth, bs, ts, H_B, D_B))
```

```python
import functools

import jax
import jax.numpy as jnp
from jax import lax
from jax.experimental import pallas as pl
from jax.experimental.pallas import tpu as pltpu

D_MODEL = 1024
DEPTH = 4
CHUNK = 64
H_A = 8
Q_LORA = 768
KV_LORA = 256
NOPE_DIM = 64
ROPE_DIM = 32
V_DIM = 64
ROPE_BASE = 10000.0
MLA_SCALE = (NOPE_DIM + ROPE_DIM) ** -0.5
LOG2E = 1.4426950408889634
H_B = 8
D_B = 64
LEFT_CHUNKS = 8
MAX_REL = 128
BAND_SCALE = D_B ** -0.5
D_FF = 2816
ALPHA = (2 * DEPTH) ** 0.25
NORM_EPS = 1e-5
NEG_INF = -1e30

LANES = 128
MXU_WIDTH = 256
HEAD_PAD = LANES
BAND_GROUP = 4
MLA_QB = 512
ROW_TILE = 512
VMEM_LIMIT = 56 * 1024 * 1024

BF16 = jnp.bfloat16
F32 = jnp.float32


def _dot(a, b):
    return jnp.dot(a, b, preferred_element_type=F32)


def _dot_nt(a, b):
    return lax.dot_general(a, b, (((1,), (1,)), ((), ())), preferred_element_type=F32)


def _layer_norm(y, g, b):
    mu = jnp.mean(y, axis=-1, keepdims=True)
    d = y - mu
    var = jnp.mean(d * d, axis=-1, keepdims=True)
    return d * lax.rsqrt(var + NORM_EPS) * g + b


def _rms_norm(y, g):
    return y * lax.rsqrt(jnp.mean(y * y, axis=-1, keepdims=True) + NORM_EPS) * g


def _sigmoid(a):
    return 1.0 / (1.0 + jnp.exp(-a))


def _layer_spec(a, layer):
    zeros = (0,) * (a.ndim - 1)
    return pl.BlockSpec((None,) + a.shape[1:], lambda *_: (layer,) + zeros, pipeline_mode=pl.Buffered(1))


def _params(*sem):
    return pltpu.CompilerParams(dimension_semantics=sem, vmem_limit_bytes=VMEM_LIMIT)


def _ffn_chunks():
    tiles = D_FF // MXU_WIDTH
    assert tiles * MXU_WIDTH == D_FF
    first = (tiles + 1) // 2 * MXU_WIDTH
    return ((0, first), (first, D_FF - first))


def _ffn_ln_kernel(x_ref, w1_ref, w2_ref, g_ref, b_ref, o_ref):
    x = x_ref[...]
    xb = x.astype(BF16)
    acc = None
    for c0, ck in _ffn_chunks():
        a = _dot(xb, w1_ref[:, c0:c0 + ck])
        g = _dot(xb, w1_ref[:, D_FF + c0:D_FF + c0 + ck])
        h = (a * _sigmoid(a) * g).astype(BF16)
        part = _dot(h, w2_ref[c0:c0 + ck, :])
        acc = part if acc is None else acc + part
    o_ref[...] = _layer_norm(ALPHA * x + 0.5 * acc, g_ref[...], b_ref[...])


def _ffn_ln(x, w1, w2, g, b, layer):
    n, d = x.shape
    tm = min(ROW_TILE, n)
    row = pl.BlockSpec((tm, d), lambda i: (i, 0))
    return pl.pallas_call(
        _ffn_ln_kernel,
        grid=(n // tm,),
        in_specs=[row] + [_layer_spec(a, layer) for a in (w1, w2, g, b)],
        out_specs=row,
        out_shape=jax.ShapeDtypeStruct((n, d), F32),
        compiler_params=_params("parallel"),
    )(x, w1, w2, g, b)


N_PROJ_IN = 14
N_PROJ_CACHE = 4


def _proj_kernel(*refs):
    (x_ref, cq_ref, sq_ref, ckr_ref, wq_ref, wckv_ref, wkr_ref, wband_ref, qg_ref, kvg_ref,
     wqa_ref, wqr_ref, wkc_ref, wuv_ref) = refs[:N_PROJ_IN]
    (qcat_ref, kcat_ref, v_ref, qb_ref, kb_ref, vb_ref,
     ckv_ref, kr_ref, kbt_ref, vbt_ref) = refs[len(refs) - 6 - N_PROJ_CACHE:]
    xb = x_ref[...].astype(BF16)
    qn = _rms_norm(_dot(xb, wq_ref[...]), qg_ref[...]).astype(BF16)
    qa = _dot(qn, wqa_ref[...])
    qr = _dot(qn, wqr_ref[...])
    cq = cq_ref[...]
    sq = sq_ref[...]
    per_vreg = LANES // ROPE_DIM
    for h in range(H_A):
        sl = slice(h * HEAD_PAD, (h + 1) * HEAD_PAD)
        src = qr[:, (h // per_vreg) * LANES:(h // per_vreg + 1) * LANES]
        shift = (NOPE_DIM - (h % per_vreg) * ROPE_DIM) % LANES
        qs = pltpu.roll(src, shift, 1) if shift else src
        qcat_ref[:, sl] = (qa[:, sl] * cq + qs * sq).astype(BF16)
    ckv = _rms_norm(_dot(xb, wckv_ref[...]), kvg_ref[...])
    ckv_ref[...] = ckv
    t = _dot(xb, wkr_ref[...]) * ckr_ref[...]
    kr = t + pltpu.roll(t, LANES - ROPE_DIM, 1)
    kr_ref[...] = kr[:, :ROPE_DIM]
    lane = lax.broadcasted_iota(jnp.int32, kr.shape, 1)
    on_rope = (lane >= NOPE_DIM) & (lane < NOPE_DIM + ROPE_DIM)
    kr_placed = jnp.where(on_rope, pltpu.roll(kr, NOPE_DIM, 1), 0.0)
    ckvb = ckv.astype(BF16)
    kc = _dot(ckvb, wkc_ref[...])
    for h in range(H_A):
        sl = slice(h * HEAD_PAD, (h + 1) * HEAD_PAD)
        kcat_ref[:, sl] = (kc[:, sl] + kr_placed).astype(BF16)
    v_ref[...] = _dot(ckvb, wuv_ref[...]).astype(BF16)
    hb = H_B * D_B
    band = _dot(xb, wband_ref[...])
    qb_ref[...] = (band[:, :hb] * (BAND_SCALE * LOG2E)).astype(BF16)
    kb = band[:, hb:2 * hb]
    vb = band[:, 2 * hb:]
    kb_ref[...] = kb.astype(BF16)
    vb_ref[...] = vb.astype(BF16)
    kbt_ref[...] = kb
    vbt_ref[...] = vb


def _proj(x, tabs, w, batch, layer, depth, caches):
    n, d = x.shape
    tm = min(ROW_TILE, n)
    ntiles = n // tm
    s = n // batch
    keep = min(LEFT_CHUNKS * CHUNK, s)
    cq, sq, ckr = tabs
    period = cq.shape[0] // tm
    if s <= tm:
        assert keep == s
        tail_blocks, tail_idx = ntiles, (lambda i: i)
    else:
        per_seq, ntail = s // tm, keep // tm
        assert per_seq * tm == s and ntail * tm == keep
        tail_blocks = batch * ntail
        tail_idx = lambda i: (i // per_seq) * ntail + jnp.maximum(i % per_seq - (per_seq - ntail), 0)
    row = lambda c: pl.BlockSpec((tm, c), lambda i: (i, 0))
    tab = lambda c: pl.BlockSpec((tm, c), lambda i: (i % period, 0))
    stacked = lambda c: pl.BlockSpec((tm, c), lambda i: (layer * ntiles + i, 0))
    tail = lambda c: pl.BlockSpec((tm, c), lambda i: (layer * tail_blocks + tail_idx(i), 0))
    hp = H_A * HEAD_PAD
    hb = H_B * D_B
    out_shape = [
        jax.ShapeDtypeStruct((n, hp), BF16),
        jax.ShapeDtypeStruct((n, hp), BF16),
        jax.ShapeDtypeStruct((n, H_A * V_DIM), BF16),
        jax.ShapeDtypeStruct((n, hb), BF16),
        jax.ShapeDtypeStruct((n, hb), BF16),
        jax.ShapeDtypeStruct((n, hb), BF16),
        jax.ShapeDtypeStruct((depth * n, KV_LORA), F32),
        jax.ShapeDtypeStruct((depth * n, ROPE_DIM), F32),
        jax.ShapeDtypeStruct((depth * tail_blocks * tm, hb), F32),
        jax.ShapeDtypeStruct((depth * tail_blocks * tm, hb), F32),
    ]
    out_specs = [row(hp), row(hp), row(H_A * V_DIM), row(hb), row(hb), row(hb),
                 stacked(KV_LORA), stacked(ROPE_DIM), tail(hb), tail(hb)]
    weights = (w["w_q"], w["w_ckv"], w["w_kr2"], w["w_band"], w["q_g"], w["kv_g"],
               w["w_qa"], w["w_qr"], w["w_kc"], w["w_uv"])
    in_specs = ([row(d), tab(HEAD_PAD), tab(HEAD_PAD), tab(LANES)]
                + [_layer_spec(a, layer) for a in weights])
    args = [x, cq, sq, ckr, *weights]
    assert len(args) == N_PROJ_IN
    aliases = {}
    if caches is not None:
        in_specs += [pl.BlockSpec(memory_space=pl.ANY)] * N_PROJ_CACHE
        args += list(caches)
        aliases = {N_PROJ_IN + k: 6 + k for k in range(N_PROJ_CACHE)}
    outs = pl.pallas_call(
        _proj_kernel,
        grid=(ntiles,),
        in_specs=in_specs,
        out_specs=out_specs,
        out_shape=out_shape,
        input_output_aliases=aliases,
        compiler_params=_params("arbitrary"),
    )(*args)
    return outs[:6], tuple(outs[6:])


def _mla_prompt_kernel(q_lo_ref, q_hi_ref, k_ref, v_ref, o_ref):
    qb = MLA_QB
    rc = lax.broadcasted_iota(jnp.int32, (qb, qb), 0) // CHUNK
    cc = lax.broadcasted_iota(jnp.int32, (qb, qb), 1) // CHUNK
    diag_mask = cc <= rc
    first_half = lax.broadcasted_iota(jnp.int32, (qb, 2 * V_DIM), 1) < V_DIM

    def head(q_ref, h, kv0):
        hs = slice(h * HEAD_PAD, (h + 1) * HEAD_PAD)
        vs = slice((h // 2) * 2 * V_DIM, (h // 2 + 1) * 2 * V_DIM)
        q = q_ref[:, hs]
        s_d = jnp.where(diag_mask, _dot_nt(q, k_ref[kv0:kv0 + qb, hs]), NEG_INF)
        m = jnp.max(s_d, axis=-1, keepdims=True)
        if kv0:
            s_f = _dot_nt(q, k_ref[0:kv0, hs])
            m = jnp.maximum(m, jnp.max(s_f, axis=-1, keepdims=True))
        p_d = jnp.exp2(s_d - m)
        l = jnp.sum(p_d, axis=-1, keepdims=True)
        o = _dot(p_d.astype(BF16), v_ref[kv0:kv0 + qb, vs])
        if kv0:
            p_f = jnp.exp2(s_f - m)
            l = l + jnp.sum(p_f, axis=-1, keepdims=True)
            o = o + _dot(p_f.astype(BF16), v_ref[0:kv0, vs])
        return o / l

    def query_block(q_ref, n):
        for hp in range(H_A // 2):
            even = head(q_ref, 2 * hp, n * qb)
            odd = head(q_ref, 2 * hp + 1, n * qb)
            o_ref[n * qb:(n + 1) * qb, hp * 2 * V_DIM:(hp + 1) * 2 * V_DIM] = (
                jnp.where(first_half, even, odd).astype(BF16))

    def variant(j):
        query_block(q_lo_ref, j)
        query_block(q_hi_ref, nq - 1 - j)

    nq = k_ref.shape[0] // qb
    for j in range(nq // 2):
        pl.when(pl.program_id(1) == j)(functools.partial(variant, j))


def _mla_prompt(qcat, kcat, v, batch):
    n = qcat.shape[0]
    s = n // batch
    nq = s // MLA_QB
    assert nq % 2 == 0
    seq = lambda c: pl.BlockSpec((s, c), lambda b, j: (b, 0))
    q_lo = pl.BlockSpec((MLA_QB, qcat.shape[1]), lambda b, j: (b * nq + j, 0))
    q_hi = pl.BlockSpec((MLA_QB, qcat.shape[1]), lambda b, j: (b * nq + nq - 1 - j, 0))
    return pl.pallas_call(
        _mla_prompt_kernel,
        grid=(batch, nq // 2),
        in_specs=[q_lo, q_hi, seq(kcat.shape[1]), seq(v.shape[1])],
        out_specs=seq(v.shape[1]),
        out_shape=jax.ShapeDtypeStruct((n, v.shape[1]), BF16),
        compiler_params=_params("parallel", "arbitrary"),
    )(qcat, qcat, kcat, v)


def _mla_sample_kernel(q_ref, ckv_ref, kr_ref, cckv_ref, ckr_ref, wabs_ref, wrope_ref, wuv_ref, o_ref):
    t = q_ref.shape[0]
    q_abs = jnp.concatenate(
        [_dot(q_ref[:, h * HEAD_PAD:(h + 1) * HEAD_PAD], wabs_ref[h]) for h in range(H_A)], axis=0).astype(BF16)
    q_rope = jnp.concatenate(
        [_dot(q_ref[:, h * HEAD_PAD:(h + 1) * HEAD_PAD], wrope_ref[h]) for h in range(H_A)], axis=0).astype(BF16)
    c_old = cckv_ref[...].astype(BF16)
    r_old = ckr_ref[...].astype(BF16)
    c_new = ckv_ref[...].astype(BF16)
    r_new = kr_ref[...].astype(BF16)
    s_old = _dot_nt(q_abs, c_old) + _dot_nt(q_rope, r_old)
    s_new = _dot_nt(q_abs, c_new) + _dot_nt(q_rope, r_new)
    m = jnp.maximum(jnp.max(s_old, axis=-1, keepdims=True), jnp.max(s_new, axis=-1, keepdims=True))
    p_old = jnp.exp2(s_old - m)
    p_new = jnp.exp2(s_new - m)
    l = jnp.sum(p_old, axis=-1, keepdims=True) + jnp.sum(p_new, axis=-1, keepdims=True)
    o_lat = ((_dot(p_old.astype(BF16), c_old) + _dot(p_new.astype(BF16), c_new)) / l).astype(BF16)
    group = lax.broadcasted_iota(jnp.int32, (t, H_A * V_DIM), 1) // V_DIM
    out = jnp.zeros((t, H_A * V_DIM), F32)
    for h in range(H_A):
        out = jnp.where(group == h, _dot(o_lat[h * t:(h + 1) * t], wuv_ref[...]), out)
    o_ref[...] = out.astype(BF16)


def _mla_sample(qcat, ckv_all, kr_all, cache_ckv, cache_kr, w, batch, layer):
    n = qcat.shape[0]
    t = n // batch
    past = cache_ckv.shape[2]
    row = lambda c: pl.BlockSpec((t, c), lambda b: (b, 0))
    new = lambda c: pl.BlockSpec((t, c), lambda b: (layer * batch + b, 0))
    old = lambda c: pl.BlockSpec((None, None, past, c), lambda b: (layer, b, 0, 0))
    weights = (w["w_abs"], w["w_ropesel"], w["w_uv"])
    return pl.pallas_call(
        _mla_sample_kernel,
        grid=(batch,),
        in_specs=[row(qcat.shape[1]), new(KV_LORA), new(ROPE_DIM), old(KV_LORA), old(ROPE_DIM)]
                 + [_layer_spec(a, layer) for a in weights],
        out_specs=row(H_A * V_DIM),
        out_shape=jax.ShapeDtypeStruct((n, H_A * V_DIM), BF16),
        compiler_params=_params("parallel"),
    )(qcat, ckv_all, kr_all, cache_ckv, cache_kr, *weights)


def _band_dims(s_len):
    group = min(BAND_GROUP, s_len // CHUNK)
    window = -(-(group + LEFT_CHUNKS) * CHUNK // LANES) * LANES
    return group, window, window - group * CHUNK


def _band_kernel(*refs, hist_rows, group, window, front):
    if hist_rows:
        tab_ref, q_ref, k_ref, v_ref, ck_ref, cv_ref, o_ref, kpad, vpad, bias = refs
    else:
        tab_ref, q_ref, k_ref, v_ref, o_ref, kpad, vpad, bias = refs
    s_len = q_ref.shape[0]
    hb = H_B * D_B
    gq = group * CHUNK
    span = (LEFT_CHUNKS + 1) * CHUNK

    @pl.when(pl.program_id(0) == 0)
    def _():
        kpad[0:front - hist_rows, :] = jnp.zeros((front - hist_rows, hb), BF16)
        vpad[0:front - hist_rows, :] = jnp.zeros((front - hist_rows, hb), BF16)
        qi = lax.broadcasted_iota(jnp.int32, (CHUNK, window), 0)
        u = lax.broadcasted_iota(jnp.int32, (CHUNK, window), 1)
        idx = jnp.clip(LEFT_CHUNKS * CHUNK + qi - u, -MAX_REL, MAX_REL) + MAX_REL
        lo = max(LEFT_CHUNKS * CHUNK - (span - 1), -MAX_REL) + MAX_REL
        for h in range(H_B):
            def fill(d, b, h=h):
                return jnp.where(idx == d, tab_ref[h, d] * LOG2E, b)
            base = lax.fori_loop(lo, 2 * MAX_REL + 1, fill, jnp.zeros((CHUNK, window), F32))
            base = jnp.where(u < span, base, NEG_INF)
            for a in range(group):
                off = front + (a - LEFT_CHUNKS) * CHUNK
                r = (h % 2) * gq + a * CHUNK
                bias[h // 2, r:r + CHUNK, :] = pltpu.roll(base, off, 1) if off else base

    if hist_rows:
        kpad[front - hist_rows:front, :] = ck_ref[...].astype(BF16)
        vpad[front - hist_rows:front, :] = cv_ref[...].astype(BF16)
    kpad[front:front + s_len, :] = k_ref[...]
    vpad[front:front + s_len, :] = v_ref[...]

    slot = lax.broadcasted_iota(jnp.int32, (1, window), 1)
    first_half = lax.broadcasted_iota(jnp.int32, (gq, 2 * D_B), 1) < D_B

    def query_group(g, _, check_exists):
        r0 = pl.multiple_of(g * gq, gq)
        q = q_ref[pl.ds(r0, gq), :]
        kw = kpad[pl.ds(r0, window), :]
        vw = vpad[pl.ds(r0, window), :]
        exists = slot >= front - hist_rows - r0
        for hp in range(H_B // 2):
            sl = slice(hp * 2 * D_B, (hp + 1) * 2 * D_B)
            qp, kp, vp = q[:, sl], kw[:, sl], vw[:, sl]
            zero = jnp.zeros_like(qp)
            q2 = jnp.concatenate([jnp.where(first_half, qp, zero), jnp.where(first_half, zero, qp)], axis=0)
            s = _dot_nt(q2, kp) + bias[hp]
            if check_exists:
                s = jnp.where(exists, s, NEG_INF)
            p = jnp.exp2(s - jnp.max(s, axis=-1, keepdims=True))
            l = jnp.sum(p, axis=-1, keepdims=True)
            o = _dot(p.astype(BF16), vp) / l
            o_ref[pl.ds(r0, gq), sl] = jnp.where(first_half, o[:gq], o[gq:]).astype(BF16)
        return 0

    n_groups = s_len // gq
    n_checked = min(n_groups, -(-(front - hist_rows) // gq))
    lax.fori_loop(0, n_checked, functools.partial(query_group, check_exists=True), 0)
    lax.fori_loop(n_checked, n_groups, functools.partial(query_group, check_exists=False), 0)


def _band(table, qb, kb, vb, batch, layer, cache_k=None, cache_v=None):
    n, hb = qb.shape
    s = n // batch
    hist_rows = 0 if cache_k is None else cache_k.shape[2]
    group, window, front = _band_dims(s)
    row = pl.BlockSpec((s, hb), lambda b: (b, 0))
    in_specs = [pl.BlockSpec(memory_space=pltpu.SMEM), row, row, row]
    args = [table, qb, kb, vb]
    if hist_rows:
        cache = pl.BlockSpec((None, None, hist_rows, hb), lambda b: (layer, b, 0, 0))
        in_specs += [cache, cache]
        args += [cache_k, cache_v]
    return pl.pallas_call(
        functools.partial(_band_kernel, hist_rows=hist_rows, group=group, window=window, front=front),
        grid=(batch,),
        in_specs=in_specs,
        out_specs=row,
        out_shape=jax.ShapeDtypeStruct((n, hb), BF16),
        scratch_shapes=[pltpu.VMEM((front + s, hb), BF16), pltpu.VMEM((front + s, hb), BF16),
                        pltpu.VMEM((H_B // 2, 2 * group * CHUNK, window), F32)],
        compiler_params=_params("arbitrary"),
    )(*args)


def _merge_kernel(x_ref, oa_ref, ob_ref, wga_ref, wgb_ref, wpa_ref, wpb_ref, wout_ref, g_ref, b_ref, o_ref):
    x = x_ref[...]
    xb = x.astype(BF16)
    mix = (_sigmoid(_dot(xb, wga_ref[...])) * _dot(oa_ref[...], wpa_ref[...])
           + _sigmoid(_dot(xb, wgb_ref[...])) * _dot(ob_ref[...], wpb_ref[...]))
    y = ALPHA * x + _dot(mix.astype(BF16), wout_ref[...])
    o_ref[...] = _layer_norm(y, g_ref[...], b_ref[...])


def _merge(x, oa, ob, w, layer):
    n, d = x.shape
    tm = min(ROW_TILE, n)
    row = lambda c: pl.BlockSpec((tm, c), lambda i: (i, 0))
    weights = (w["w_ga"], w["w_gb"], w["w_pa"], w["w_pb"], w["w_out"], w["ln2_g"], w["ln2_b"])
    return pl.pallas_call(
        _merge_kernel,
        grid=(n // tm,),
        in_specs=[row(d), row(oa.shape[1]), row(ob.shape[1])] + [_layer_spec(a, layer) for a in weights],
        out_specs=row(d),
        out_shape=jax.ShapeDtypeStruct((n, d), F32),
        compiler_params=_params("parallel"),
    )(x, oa, ob, *weights)


def _rope_tables(pos, rows):
    half = ROPE_DIM // 2
    inv = ROPE_BASE ** (-jnp.arange(half, dtype=F32) / half)
    ang = pos.astype(F32)[:, None] * inv[None, :]
    cos, sin = jnp.cos(ang), jnp.sin(ang)
    ck = jnp.concatenate([cos, cos], axis=-1)
    sk = jnp.concatenate([-sin, sin], axis=-1)
    t = pos.shape[0]
    pad = jnp.zeros((t, HEAD_PAD - NOPE_DIM - ROPE_DIM), F32)
    cq = jnp.concatenate([jnp.ones((t, NOPE_DIM), F32), ck, pad], axis=-1) * (MLA_SCALE * LOG2E)
    sq = jnp.concatenate([jnp.zeros((t, NOPE_DIM), F32), sk, pad], axis=-1) * (MLA_SCALE * LOG2E)
    ckr = jnp.concatenate([ck, sk, jnp.zeros((t, LANES - 2 * ROPE_DIM), F32)], axis=-1)
    rep = rows // t
    return tuple(jnp.tile(a, (rep, 1)) for a in (cq, sq, ckr))


def _swap_halves(w):
    half = w.shape[-1] // 2
    return jnp.concatenate([w[..., half:], w[..., :half]], axis=-1)


def _prep_weights(ln1_g, ln1_b, ffn1_w1, ffn1_w2, w_in, q_g, w_uq, kv_g, w_uk, w_uv, rel_bias,
                  w_pa, w_pb, w_out, ln2_g, ln2_b, ffn2_w1, ffn2_w2, ln3_g, ln3_b):
    depth = w_in.shape[0]
    hb = H_B * D_B
    c0, c1, c2 = Q_LORA, Q_LORA + KV_LORA, Q_LORA + KV_LORA + ROPE_DIM
    c3 = c2 + 3 * hb
    w_kr = w_in[:, :, c1:c2]
    qd = NOPE_DIM + ROPE_DIM
    uq = w_uq.reshape(depth, Q_LORA, H_A, qd)
    zq = jnp.zeros((depth, Q_LORA, H_A, HEAD_PAD - qd), F32)
    w_qa = jnp.concatenate([uq, zq], axis=-1).reshape(depth, Q_LORA, H_A * HEAD_PAD)
    w_qr = _swap_halves(uq[..., NOPE_DIM:]).reshape(depth, Q_LORA, H_A * ROPE_DIM)
    uk = w_uk.reshape(depth, KV_LORA, H_A, NOPE_DIM)
    w_kc = jnp.concatenate([uk, jnp.zeros((depth, KV_LORA, H_A, HEAD_PAD - NOPE_DIM), F32)],
                           axis=-1).reshape(depth, KV_LORA, H_A * HEAD_PAD)
    eye = jnp.eye(ROPE_DIM, dtype=F32)
    place = jnp.concatenate([jnp.zeros((ROPE_DIM, NOPE_DIM), F32), eye,
                             jnp.zeros((ROPE_DIM, HEAD_PAD - qd), F32)], axis=-1)
    w_abs = jnp.concatenate([jnp.transpose(uk, (0, 2, 3, 1)),
                             jnp.zeros((depth, H_A, HEAD_PAD - NOPE_DIM, KV_LORA), F32)], axis=2)
    w_ropesel = jnp.broadcast_to(place.T, (depth, H_A, HEAD_PAD, ROPE_DIM))
    bf = lambda a: a.astype(BF16)
    row = lambda a: a.reshape(depth, 1, -1)
    return dict(
        ln1_g=row(ln1_g), ln1_b=row(ln1_b), f1_w1=bf(ffn1_w1), f1_w2=bf(ffn1_w2),
        w_q=bf(w_in[:, :, :c0]), w_ckv=bf(w_in[:, :, c0:c1]),
        w_kr2=bf(jnp.concatenate([w_kr, _swap_halves(w_kr),
                                  jnp.zeros((depth, D_MODEL, LANES - 2 * ROPE_DIM), F32)], axis=-1)),
        w_band=bf(w_in[:, :, c2:c3]), w_ga=bf(w_in[:, :, c3:c3 + D_MODEL]), w_gb=bf(w_in[:, :, c3 + D_MODEL:]),
        q_g=row(q_g), kv_g=row(kv_g), w_qa=bf(w_qa), w_qr=bf(w_qr), w_kc=bf(w_kc),
        w_uv=bf(w_uv), w_abs=bf(w_abs), w_ropesel=bf(w_ropesel), rel_bias=rel_bias,
        w_pa=bf(w_pa), w_pb=bf(w_pb), w_out=bf(w_out), ln2_g=row(ln2_g), ln2_b=row(ln2_b),
        f2_w1=bf(ffn2_w1), f2_w2=bf(ffn2_w2), ln3_g=row(ln3_g), ln3_b=row(ln3_b),
    )


def _layer(x, tabs, w, batch, layer, depth, new_caches, old_caches=None):
    x = _ffn_ln(x, w["f1_w1"], w["f1_w2"], w["ln1_g"], w["ln1_b"], layer)
    (qcat, kcat, v, qb, kb, vb), new_caches = _proj(x, tabs, w, batch, layer, depth, new_caches)
    table = w["rel_bias"][layer]
    if old_caches is None:
        oa = _mla_prompt(qcat, kcat, v, batch)
        ob = _band(table, qb, kb, vb, batch, layer)
    else:
        c_ckv, c_kr, c_k, c_v = old_caches
        oa = _mla_sample(qcat, new_caches[0], new_caches[1], c_ckv, c_kr, w, batch, layer)
        ob = _band(table, qb, kb, vb, batch, layer, c_k, c_v)
    x = _merge(x, oa, ob, w, layer)
    x = _ffn_ln(x, w["f2_w1"], w["f2_w2"], w["ln3_g"], w["ln3_b"], layer)
    return x, new_caches


def kernel(x_prompt, x_sample, cache_mla_ckv, cache_mla_krope, cache_band_k, cache_band_v, ln1_g, ln1_b, ffn1_w1, ffn1_w2, w_in, mla_q_norm_g, mla_w_uq, mla_kv_norm_g, mla_w_uk, mla_w_uv, band_rel_bias, w_proj_a, w_proj_b, w_out, ln2_g, ln2_b, ffn2_w1, ffn2_w2, ln3_g, ln3_b):
    bp, sp, d = x_prompt.shape
    bs, ts, _ = x_sample.shape
    depth = ln1_g.shape[0]
    past = cache_mla_ckv.shape[2]
    hist = cache_band_k.shape[2]
    hb = H_B * D_B
    assert d == D_MODEL and sp % MLA_QB == 0 and ts == CHUNK and hist == LEFT_CHUNKS * CHUNK
    np_, ns = bp * sp, bs * ts
    tabs_p = _rope_tables(jnp.arange(sp, dtype=jnp.int32), max(sp, min(ROW_TILE, np_)))
    tabs_s = _rope_tables(past + jnp.arange(ts, dtype=jnp.int32), min(ROW_TILE, ns))
    w = _prep_weights(ln1_g, ln1_b, ffn1_w1, ffn1_w2, w_in, mla_q_norm_g, mla_w_uq, mla_kv_norm_g,
                      mla_w_uk, mla_w_uv, band_rel_bias, w_proj_a, w_proj_b, w_out, ln2_g, ln2_b,
                      ffn2_w1, ffn2_w2, ln3_g, ln3_b)
    old = (cache_mla_ckv, cache_mla_krope,
           cache_band_k.reshape(depth, bs, hist, hb), cache_band_v.reshape(depth, bs, hist, hb))
    xp = x_prompt.reshape(np_, d)
    xs = x_sample.reshape(ns, d)
    new_p = new_s = None
    for l in range(depth):
        xp, new_p = _layer(xp, tabs_p, w, bp, l, depth, new_p)
        xs, new_s = _layer(xs, tabs_s, w, bs, l, depth, new_s, old)
    keep = min(LEFT_CHUNKS * CHUNK, sp)
    return (xp.reshape(bp, sp, d), xs.reshape(bs, ts, d),
            new_p[0].reshape(depth, bp, sp, KV_LORA), new_p[1].reshape(depth, bp, sp, ROPE_DIM),
            new_p[2].reshape(depth, bp, keep, H_B, D_B), new_p[3].reshape(depth, bp, keep, H_B, D_B),
            new_s[0].reshape(depth, bs, ts, KV_LORA), new_s[1].reshape(depth, bs, ts, ROPE_DIM),
            new_s[2].reshape(depth, bs, ts, H_B, D_B), new_s[3].reshape(depth, bs, ts, H_B, D_B))
```

```python
import functools

import jax
import jax.numpy as jnp
from jax import lax
from jax.experimental import pallas as pl
from jax.experimental.pallas import tpu as pltpu

D_MODEL = 1024
DEPTH = 4
CHUNK = 64
H_A = 8
Q_LORA = 768
KV_LORA = 256
NOPE_DIM = 64
ROPE_DIM = 32
V_DIM = 64
ROPE_BASE = 10000.0
MLA_SCALE = (NOPE_DIM + ROPE_DIM) ** -0.5
LOG2E = 1.4426950408889634
H_B = 8
D_B = 64
LEFT_CHUNKS = 8
MAX_REL = 128
BAND_SCALE = D_B ** -0.5
D_FF = 2816
ALPHA = (2 * DEPTH) ** 0.25
NORM_EPS = 1e-5
NEG_INF = -1e30

LANES = 128
MXU_WIDTH = 256
HEAD_PAD = LANES
BAND_GROUP = 4
MLA_QB = 512
ROW_TILE = 512
VMEM_LIMIT = 56 * 1024 * 1024

BF16 = jnp.bfloat16
F32 = jnp.float32


def _dot(a, b):
    return jnp.dot(a, b, preferred_element_type=F32)


def _dot_nt(a, b):
    return lax.dot_general(a, b, (((1,), (1,)), ((), ())), preferred_element_type=F32)


def _layer_norm(y, g, b):
    mu = jnp.mean(y, axis=-1, keepdims=True)
    d = y - mu
    var = jnp.mean(d * d, axis=-1, keepdims=True)
    return d * lax.rsqrt(var + NORM_EPS) * g + b


def _rms_norm(y, g):
    return y * lax.rsqrt(jnp.mean(y * y, axis=-1, keepdims=True) + NORM_EPS) * g


def _sigmoid(a):
    return 1.0 / (1.0 + jnp.exp(-a))


def _layer_spec(a, layer):
    zeros = (0,) * (a.ndim - 1)
    return pl.BlockSpec((None,) + a.shape[1:], lambda *_: (layer,) + zeros, pipeline_mode=pl.Buffered(1))


def _params(*sem):
    return pltpu.CompilerParams(dimension_semantics=sem, vmem_limit_bytes=VMEM_LIMIT)


def _ffn_chunks():
    tiles = D_FF // MXU_WIDTH
    assert tiles * MXU_WIDTH == D_FF
    first = (tiles + 1) // 2 * MXU_WIDTH
    return ((0, first), (first, D_FF - first))


def _ffn_ln_kernel(x_ref, w1_ref, w2_ref, g_ref, b_ref, o_ref):
    x = x_ref[...]
    xb = x.astype(BF16)
    acc = None
    for c0, ck in _ffn_chunks():
        a = _dot(xb, w1_ref[:, c0:c0 + ck])
        g = _dot(xb, w1_ref[:, D_FF + c0:D_FF + c0 + ck])
        h = (a * _sigmoid(a) * g).astype(BF16)
        part = _dot(h, w2_ref[c0:c0 + ck, :])
        acc = part if acc is None else acc + part
    o_ref[...] = _layer_norm(ALPHA * x + 0.5 * acc, g_ref[...], b_ref[...])


def _ffn_ln(x, w1, w2, g, b, layer):
    n, d = x.shape
    tm = min(ROW_TILE, n)
    row = pl.BlockSpec((tm, d), lambda i: (i, 0))
    return pl.pallas_call(
        _ffn_ln_kernel,
        grid=(n // tm,),
        in_specs=[row] + [_layer_spec(a, layer) for a in (w1, w2, g, b)],
        out_specs=row,
        out_shape=jax.ShapeDtypeStruct((n, d), F32),
        compiler_params=_params("parallel"),
    )(x, w1, w2, g, b)


N_PROJ_IN = 14
N_PROJ_CACHE = 4


def _proj_kernel(*refs):
    (x_ref, cq_ref, sq_ref, ckr_ref, wq_ref, wckv_ref, wkr_ref, wband_ref, qg_ref, kvg_ref,
     wqa_ref, wqr_ref, wkc_ref, wuv_ref) = refs[:N_PROJ_IN]
    (qcat_ref, kcat_ref, v_ref, qb_ref, kb_ref, vb_ref,
     ckv_ref, kr_ref, kbt_ref, vbt_ref) = refs[len(refs) - 6 - N_PROJ_CACHE:]
    xb = x_ref[...].astype(BF16)
    qn = _rms_norm(_dot(xb, wq_ref[...]), qg_ref[...]).astype(BF16)
    qa = _dot(qn, wqa_ref[...])
    qr = _dot(qn, wqr_ref[...])
    cq = cq_ref[...]
    sq = sq_ref[...]
    per_vreg = LANES // ROPE_DIM
    for h in range(H_A):
        sl = slice(h * HEAD_PAD, (h + 1) * HEAD_PAD)
        src = qr[:, (h // per_vreg) * LANES:(h // per_vreg + 1) * LANES]
        shift = (NOPE_DIM - (h % per_vreg) * ROPE_DIM) % LANES
        qs = pltpu.roll(src, shift, 1) if shift else src
        qcat_ref[:, sl] = (qa[:, sl] * cq + qs * sq).astype(BF16)
    ckv = _rms_norm(_dot(xb, wckv_ref[...]), kvg_ref[...])
    ckv_ref[...] = ckv
    t = _dot(xb, wkr_ref[...]) * ckr_ref[...]
    kr = t + pltpu.roll(t, LANES - ROPE_DIM, 1)
    kr_ref[...] = kr[:, :ROPE_DIM]
    lane = lax.broadcasted_iota(jnp.int32, kr.shape, 1)
    on_rope = (lane >= NOPE_DIM) & (lane < NOPE_DIM + ROPE_DIM)
    kr_placed = jnp.where(on_rope, pltpu.roll(kr, NOPE_DIM, 1), 0.0)
    ckvb = ckv.astype(BF16)
    kc = _dot(ckvb, wkc_ref[...])
    for h in range(H_A):
        sl = slice(h * HEAD_PAD, (h + 1) * HEAD_PAD)
        kcat_ref[:, sl] = (kc[:, sl] + kr_placed).astype(BF16)
    v_ref[...] = _dot(ckvb, wuv_ref[...]).astype(BF16)
    hb = H_B * D_B
    band = _dot(xb, wband_ref[...])
    qb_ref[...] = (band[:, :hb] * (BAND_SCALE * LOG2E)).astype(BF16)
    kb = band[:, hb:2 * hb]
    vb = band[:, 2 * hb:]
    kb_ref[...] = kb.astype(BF16)
    vb_ref[...] = vb.astype(BF16)
    kbt_ref[...] = kb
    vbt_ref[...] = vb


def _proj(x, tabs, w, batch, layer, depth, caches):
    n, d = x.shape
    tm = min(ROW_TILE, n)
    ntiles = n // tm
    s = n // batch
    keep = min(LEFT_CHUNKS * CHUNK, s)
    cq, sq, ckr = tabs
    period = cq.shape[0] // tm
    if s <= tm:
        assert keep == s
        tail_blocks, tail_idx = ntiles, (lambda i: i)
    else:
        per_seq, ntail = s // tm, keep // tm
        assert per_seq * tm == s and ntail * tm == keep
        tail_blocks = batch * ntail
        tail_idx = lambda i: (i // per_seq) * ntail + jnp.maximum(i % per_seq - (per_seq - ntail), 0)
    row = lambda c: pl.BlockSpec((tm, c), lambda i: (i, 0))
    tab = lambda c: pl.BlockSpec((tm, c), lambda i: (i % period, 0))
    stacked = lambda c: pl.BlockSpec((tm, c), lambda i: (layer * ntiles + i, 0))
    tail = lambda c: pl.BlockSpec((tm, c), lambda i: (layer * tail_blocks + tail_idx(i), 0))
    hp = H_A * HEAD_PAD
    hb = H_B * D_B
    out_shape = [
        jax.ShapeDtypeStruct((n, hp), BF16),
        jax.ShapeDtypeStruct((n, hp), BF16),
        jax.ShapeDtypeStruct((n, H_A * V_DIM), BF16),
        jax.ShapeDtypeStruct((n, hb), BF16),
        jax.ShapeDtypeStruct((n, hb), BF16),
        jax.ShapeDtypeStruct((n, hb), BF16),
        jax.ShapeDtypeStruct((depth * n, KV_LORA), F32),
        jax.ShapeDtypeStruct((depth * n, ROPE_DIM), F32),
        jax.ShapeDtypeStruct((depth * tail_blocks * tm, hb), F32),
        jax.ShapeDtypeStruct((depth * tail_blocks * tm, hb), F32),
    ]
    out_specs = [row(hp), row(hp), row(H_A * V_DIM), row(hb), row(hb), row(hb),
                 stacked(KV_LORA), stacked(ROPE_DIM), tail(hb), tail(hb)]
    weights = (w["w_q"], w["w_ckv"], w["w_kr2"], w["w_band"], w["q_g"], w["kv_g"],
               w["w_qa"], w["w_qr"], w["w_kc"], w["w_uv"])
    in_specs = ([row(d), tab(HEAD_PAD), tab(HEAD_PAD), tab(LANES)]
                + [_layer_spec(a, layer) for a in weights])
    args = [x, cq, sq, ckr, *weights]
    assert len(args) == N_PROJ_IN
    aliases = {}
    if caches is not None:
        in_specs += [pl.BlockSpec(memory_space=pl.ANY)] * N_PROJ_CACHE
        args += list(caches)
        aliases = {N_PROJ_IN + k: 6 + k for k in range(N_PROJ_CACHE)}
    outs = pl.pallas_call(
        _proj_kernel,
        grid=(ntiles,),
        in_specs=in_specs,
        out_specs=out_specs,
        out_shape=out_shape,
        input_output_aliases=aliases,
        compiler_params=_params("arbitrary"),
    )(*args)
    return outs[:6], tuple(outs[6:])


def _mla_prompt_kernel(q_lo_ref, q_hi_ref, k_ref, v_ref, o_ref):
    qb = MLA_QB
    rc = lax.broadcasted_iota(jnp.int32, (qb, qb), 0) // CHUNK
    cc = lax.broadcasted_iota(jnp.int32, (qb, qb), 1) // CHUNK
    diag_mask = cc <= rc
    first_half = lax.broadcasted_iota(jnp.int32, (qb, 2 * V_DIM), 1) < V_DIM

    def scores(q_ref, h, kv0):
        hs = slice(h * HEAD_PAD, (h + 1) * HEAD_PAD)
        q = q_ref[:, hs]
        s_d = jnp.where(diag_mask, _dot_nt(q, k_ref[kv0:kv0 + qb, hs]), NEG_INF)
        s_f = _dot_nt(q, k_ref[0:kv0, hs]) if kv0 else None
        return s_d, s_f

    def attend(h, kv0, s_d, s_f):
        vs = slice((h // 2) * 2 * V_DIM, (h // 2 + 1) * 2 * V_DIM)
        m = jnp.max(s_d, axis=-1, keepdims=True)
        if s_f is not None:
            m = jnp.maximum(m, jnp.max(s_f, axis=-1, keepdims=True))
        p_d = jnp.exp2(s_d - m)
        l = jnp.sum(p_d, axis=-1, keepdims=True)
        o = _dot(p_d.astype(BF16), v_ref[kv0:kv0 + qb, vs])
        if s_f is not None:
            p_f = jnp.exp2(s_f - m)
            l = l + jnp.sum(p_f, axis=-1, keepdims=True)
            o = o + _dot(p_f.astype(BF16), v_ref[0:kv0, vs])
        return o / l

    def query_block(q_ref, n):
        kv0 = n * qb
        ahead = scores(q_ref, 0, kv0)
        outs = []
        for h in range(H_A):
            cur = ahead
            if h + 1 < H_A:
                ahead = scores(q_ref, h + 1, kv0)
            outs.append(attend(h, kv0, *cur))
            if h % 2:
                hp = h // 2
                o_ref[kv0:kv0 + qb, hp * 2 * V_DIM:(hp + 1) * 2 * V_DIM] = (
                    jnp.where(first_half, outs[h - 1], outs[h]).astype(BF16))

    def variant(j):
        query_block(q_lo_ref, j)
        query_block(q_hi_ref, nq - 1 - j)

    nq = k_ref.shape[0] // qb
    for j in range(nq // 2):
        pl.when(pl.program_id(1) == j)(functools.partial(variant, j))


def _mla_prompt(qcat, kcat, v, batch):
    n = qcat.shape[0]
    s = n // batch
    nq = s // MLA_QB
    assert nq % 2 == 0
    seq = lambda c: pl.BlockSpec((s, c), lambda b, j: (b, 0))
    q_lo = pl.BlockSpec((MLA_QB, qcat.shape[1]), lambda b, j: (b * nq + j, 0))
    q_hi = pl.BlockSpec((MLA_QB, qcat.shape[1]), lambda b, j: (b * nq + nq - 1 - j, 0))
    return pl.pallas_call(
        _mla_prompt_kernel,
        grid=(batch, nq // 2),
        in_specs=[q_lo, q_hi, seq(kcat.shape[1]), seq(v.shape[1])],
        out_specs=seq(v.shape[1]),
        out_shape=jax.ShapeDtypeStruct((n, v.shape[1]), BF16),
        compiler_params=_params("parallel", "arbitrary"),
    )(qcat, qcat, kcat, v)


def _mla_sample_kernel(q_ref, ckv_ref, kr_ref, cckv_ref, ckr_ref, wabs_ref, wrope_ref, wuv_ref, o_ref):
    t = q_ref.shape[0]
    q_abs = jnp.concatenate(
        [_dot(q_ref[:, h * HEAD_PAD:(h + 1) * HEAD_PAD], wabs_ref[h]) for h in range(H_A)], axis=0).astype(BF16)
    q_rope = jnp.concatenate(
        [_dot(q_ref[:, h * HEAD_PAD:(h + 1) * HEAD_PAD], wrope_ref[h]) for h in range(H_A)], axis=0).astype(BF16)
    c_old = cckv_ref[...].astype(BF16)
    r_old = ckr_ref[...].astype(BF16)
    c_new = ckv_ref[...].astype(BF16)
    r_new = kr_ref[...].astype(BF16)
    s_old = _dot_nt(q_abs, c_old) + _dot_nt(q_rope, r_old)
    s_new = _dot_nt(q_abs, c_new) + _dot_nt(q_rope, r_new)
    m = jnp.maximum(jnp.max(s_old, axis=-1, keepdims=True), jnp.max(s_new, axis=-1, keepdims=True))
    p_old = jnp.exp2(s_old - m)
    p_new = jnp.exp2(s_new - m)
    l = jnp.sum(p_old, axis=-1, keepdims=True) + jnp.sum(p_new, axis=-1, keepdims=True)
    o_lat = ((_dot(p_old.astype(BF16), c_old) + _dot(p_new.astype(BF16), c_new)) / l).astype(BF16)
    group = lax.broadcasted_iota(jnp.int32, (t, H_A * V_DIM), 1) // V_DIM
    out = jnp.zeros((t, H_A * V_DIM), F32)
    for h in range(H_A):
        out = jnp.where(group == h, _dot(o_lat[h * t:(h + 1) * t], wuv_ref[...]), out)
    o_ref[...] = out.astype(BF16)


def _mla_sample(qcat, ckv_all, kr_all, cache_ckv, cache_kr, w, batch, layer):
    n = qcat.shape[0]
    t = n // batch
    past = cache_ckv.shape[2]
    row = lambda c: pl.BlockSpec((t, c), lambda b: (b, 0))
    new = lambda c: pl.BlockSpec((t, c), lambda b: (layer * batch + b, 0))
    old = lambda c: pl.BlockSpec((None, None, past, c), lambda b: (layer, b, 0, 0))
    weights = (w["w_abs"], w["w_ropesel"], w["w_uv"])
    return pl.pallas_call(
        _mla_sample_kernel,
        grid=(batch,),
        in_specs=[row(qcat.shape[1]), new(KV_LORA), new(ROPE_DIM), old(KV_LORA), old(ROPE_DIM)]
                 + [_layer_spec(a, layer) for a in weights],
        out_specs=row(H_A * V_DIM),
        out_shape=jax.ShapeDtypeStruct((n, H_A * V_DIM), BF16),
        compiler_params=_params("parallel"),
    )(qcat, ckv_all, kr_all, cache_ckv, cache_kr, *weights)


def _band_dims(s_len):
    group = min(BAND_GROUP, s_len // CHUNK)
    window = -(-(group + LEFT_CHUNKS) * CHUNK // LANES) * LANES
    return group, window, window - group * CHUNK


def _band_kernel(*refs, hist_rows, group, window, front):
    if hist_rows:
        tab_ref, q_ref, k_ref, v_ref, ck_ref, cv_ref, o_ref, kpad, vpad, bias = refs
    else:
        tab_ref, q_ref, k_ref, v_ref, o_ref, kpad, vpad, bias = refs
    s_len = q_ref.shape[0]
    hb = H_B * D_B
    gq = group * CHUNK
    span = (LEFT_CHUNKS + 1) * CHUNK

    @pl.when(pl.program_id(0) == 0)
    def _():
        kpad[0:front - hist_rows, :] = jnp.zeros((front - hist_rows, hb), BF16)
        vpad[0:front - hist_rows, :] = jnp.zeros((front - hist_rows, hb), BF16)
        qi = lax.broadcasted_iota(jnp.int32, (CHUNK, window), 0)
        u = lax.broadcasted_iota(jnp.int32, (CHUNK, window), 1)
        idx = jnp.clip(LEFT_CHUNKS * CHUNK + qi - u, -MAX_REL, MAX_REL) + MAX_REL
        lo = max(LEFT_CHUNKS * CHUNK - (span - 1), -MAX_REL) + MAX_REL
        for h in range(H_B):
            def fill(d, b, h=h):
                return jnp.where(idx == d, tab_ref[h, d] * LOG2E, b)
            base = lax.fori_loop(lo, 2 * MAX_REL + 1, fill, jnp.zeros((CHUNK, window), F32))
            base = jnp.where(u < span, base, NEG_INF)
            for a in range(group):
                off = front + (a - LEFT_CHUNKS) * CHUNK
                r = (h % 2) * gq + a * CHUNK
                bias[h // 2, r:r + CHUNK, :] = pltpu.roll(base, off, 1) if off else base

    if hist_rows:
        kpad[front - hist_rows:front, :] = ck_ref[...].astype(BF16)
        vpad[front - hist_rows:front, :] = cv_ref[...].astype(BF16)
    kpad[front:front + s_len, :] = k_ref[...]
    vpad[front:front + s_len, :] = v_ref[...]

    slot = lax.broadcasted_iota(jnp.int32, (1, window), 1)
    first_half = lax.broadcasted_iota(jnp.int32, (gq, 2 * D_B), 1) < D_B

    def query_group(g, _, check_exists):
        r0 = pl.multiple_of(g * gq, gq)
        q = q_ref[pl.ds(r0, gq), :]
        kw = kpad[pl.ds(r0, window), :]
        vw = vpad[pl.ds(r0, window), :]
        exists = slot >= front - hist_rows - r0
        def scores(hp):
            sl = slice(hp * 2 * D_B, (hp + 1) * 2 * D_B)
            qp = q[:, sl]
            zero = jnp.zeros_like(qp)
            q2 = jnp.concatenate([jnp.where(first_half, qp, zero), jnp.where(first_half, zero, qp)], axis=0)
            s = _dot_nt(q2, kw[:, sl]) + bias[hp]
            return jnp.where(exists, s, NEG_INF) if check_exists else s

        ahead = scores(0)
        for hp in range(H_B // 2):
            sl = slice(hp * 2 * D_B, (hp + 1) * 2 * D_B)
            s = ahead
            if hp + 1 < H_B // 2:
                ahead = scores(hp + 1)
            p = jnp.exp2(s - jnp.max(s, axis=-1, keepdims=True))
            l = jnp.sum(p, axis=-1, keepdims=True)
            o = _dot(p.astype(BF16), vw[:, sl]) / l
            o_ref[pl.ds(r0, gq), sl] = jnp.where(first_half, o[:gq], o[gq:]).astype(BF16)
        return 0

    n_groups = s_len // gq
    n_checked = min(n_groups, -(-(front - hist_rows) // gq))
    lax.fori_loop(0, n_checked, functools.partial(query_group, check_exists=True), 0)
    lax.fori_loop(n_checked, n_groups, functools.partial(query_group, check_exists=False), 0)


def _band(table, qb, kb, vb, batch, layer, cache_k=None, cache_v=None):
    n, hb = qb.shape
    s = n // batch
    hist_rows = 0 if cache_k is None else cache_k.shape[2]
    group, window, front = _band_dims(s)
    row = pl.BlockSpec((s, hb), lambda b: (b, 0))
    in_specs = [pl.BlockSpec(memory_space=pltpu.SMEM), row, row, row]
    args = [table, qb, kb, vb]
    if hist_rows:
        cache = pl.BlockSpec((None, None, hist_rows, hb), lambda b: (layer, b, 0, 0))
        in_specs += [cache, cache]
        args += [cache_k, cache_v]
    return pl.pallas_call(
        functools.partial(_band_kernel, hist_rows=hist_rows, group=group, window=window, front=front),
        grid=(batch,),
        in_specs=in_specs,
        out_specs=row,
        out_shape=jax.ShapeDtypeStruct((n, hb), BF16),
        scratch_shapes=[pltpu.VMEM((front + s, hb), BF16), pltpu.VMEM((front + s, hb), BF16),
                        pltpu.VMEM((H_B // 2, 2 * group * CHUNK, window), F32)],
        compiler_params=_params("arbitrary"),
    )(*args)


def _merge_kernel(x_ref, oa_ref, ob_ref, wga_ref, wgb_ref, wpa_ref, wpb_ref, wout_ref, g_ref, b_ref, o_ref):
    x = x_ref[...]
    xb = x.astype(BF16)
    mix = (_sigmoid(_dot(xb, wga_ref[...])) * _dot(oa_ref[...], wpa_ref[...])
           + _sigmoid(_dot(xb, wgb_ref[...])) * _dot(ob_ref[...], wpb_ref[...]))
    y = ALPHA * x + _dot(mix.astype(BF16), wout_ref[...])
    o_ref[...] = _layer_norm(y, g_ref[...], b_ref[...])


def _merge(x, oa, ob, w, layer):
    n, d = x.shape
    tm = min(ROW_TILE, n)
    row = lambda c: pl.BlockSpec((tm, c), lambda i: (i, 0))
    weights = (w["w_ga"], w["w_gb"], w["w_pa"], w["w_pb"], w["w_out"], w["ln2_g"], w["ln2_b"])
    return pl.pallas_call(
        _merge_kernel,
        grid=(n // tm,),
        in_specs=[row(d), row(oa.shape[1]), row(ob.shape[1])] + [_layer_spec(a, layer) for a in weights],
        out_specs=row(d),
        out_shape=jax.ShapeDtypeStruct((n, d), F32),
        compiler_params=_params("parallel"),
    )(x, oa, ob, *weights)


def _rope_tables(pos, rows):
    half = ROPE_DIM // 2
    inv = ROPE_BASE ** (-jnp.arange(half, dtype=F32) / half)
    ang = pos.astype(F32)[:, None] * inv[None, :]
    cos, sin = jnp.cos(ang), jnp.sin(ang)
    ck = jnp.concatenate([cos, cos], axis=-1)
    sk = jnp.concatenate([-sin, sin], axis=-1)
    t = pos.shape[0]
    pad = jnp.zeros((t, HEAD_PAD - NOPE_DIM - ROPE_DIM), F32)
    cq = jnp.concatenate([jnp.ones((t, NOPE_DIM), F32), ck, pad], axis=-1) * (MLA_SCALE * LOG2E)
    sq = jnp.concatenate([jnp.zeros((t, NOPE_DIM), F32), sk, pad], axis=-1) * (MLA_SCALE * LOG2E)
    ckr = jnp.concatenate([ck, sk, jnp.zeros((t, LANES - 2 * ROPE_DIM), F32)], axis=-1)
    rep = rows // t
    return tuple(jnp.tile(a, (rep, 1)) for a in (cq, sq, ckr))


def _swap_halves(w):
    half = w.shape[-1] // 2
    return jnp.concatenate([w[..., half:], w[..., :half]], axis=-1)


def _prep_weights(ln1_g, ln1_b, ffn1_w1, ffn1_w2, w_in, q_g, w_uq, kv_g, w_uk, w_uv, rel_bias,
                  w_pa, w_pb, w_out, ln2_g, ln2_b, ffn2_w1, ffn2_w2, ln3_g, ln3_b):
    depth = w_in.shape[0]
    hb = H_B * D_B
    c0, c1, c2 = Q_LORA, Q_LORA + KV_LORA, Q_LORA + KV_LORA + ROPE_DIM
    c3 = c2 + 3 * hb
    w_kr = w_in[:, :, c1:c2]
    qd = NOPE_DIM + ROPE_DIM
    uq = w_uq.reshape(depth, Q_LORA, H_A, qd)
    zq = jnp.zeros((depth, Q_LORA, H_A, HEAD_PAD - qd), F32)
    w_qa = jnp.concatenate([uq, zq], axis=-1).reshape(depth, Q_LORA, H_A * HEAD_PAD)
    w_qr = _swap_halves(uq[..., NOPE_DIM:]).reshape(depth, Q_LORA, H_A * ROPE_DIM)
    uk = w_uk.reshape(depth, KV_LORA, H_A, NOPE_DIM)
    w_kc = jnp.concatenate([uk, jnp.zeros((depth, KV_LORA, H_A, HEAD_PAD - NOPE_DIM), F32)],
                           axis=-1).reshape(depth, KV_LORA, H_A * HEAD_PAD)
    eye = jnp.eye(ROPE_DIM, dtype=F32)
    place = jnp.concatenate([jnp.zeros((ROPE_DIM, NOPE_DIM), F32), eye,
                             jnp.zeros((ROPE_DIM, HEAD_PAD - qd), F32)], axis=-1)
    w_abs = jnp.concatenate([jnp.transpose(uk, (0, 2, 3, 1)),
                             jnp.zeros((depth, H_A, HEAD_PAD - NOPE_DIM, KV_LORA), F32)], axis=2)
    w_ropesel = jnp.broadcast_to(place.T, (depth, H_A, HEAD_PAD, ROPE_DIM))
    bf = lambda a: a.astype(BF16)
    row = lambda a: a.reshape(depth, 1, -1)
    return dict(
        ln1_g=row(ln1_g), ln1_b=row(ln1_b), f1_w1=bf(ffn1_w1), f1_w2=bf(ffn1_w2),
        w_q=bf(w_in[:, :, :c0]), w_ckv=bf(w_in[:, :, c0:c1]),
        w_kr2=bf(jnp.concatenate([w_kr, _swap_halves(w_kr),
                                  jnp.zeros((depth, D_MODEL, LANES - 2 * ROPE_DIM), F32)], axis=-1)),
        w_band=bf(w_in[:, :, c2:c3]), w_ga=bf(w_in[:, :, c3:c3 + D_MODEL]), w_gb=bf(w_in[:, :, c3 + D_MODEL:]),
        q_g=row(q_g), kv_g=row(kv_g), w_qa=bf(w_qa), w_qr=bf(w_qr), w_kc=bf(w_kc),
        w_uv=bf(w_uv), w_abs=bf(w_abs), w_ropesel=bf(w_ropesel), rel_bias=rel_bias,
        w_pa=bf(w_pa), w_pb=bf(w_pb), w_out=bf(w_out), ln2_g=row(ln2_g), ln2_b=row(ln2_b),
        f2_w1=bf(ffn2_w1), f2_w2=bf(ffn2_w2), ln3_g=row(ln3_g), ln3_b=row(ln3_b),
    )


def _layer(x, tabs, w, batch, layer, depth, new_caches, old_caches=None):
    x = _ffn_ln(x, w["f1_w1"], w["f1_w2"], w["ln1_g"], w["ln1_b"], layer)
    (qcat, kcat, v, qb, kb, vb), new_caches = _proj(x, tabs, w, batch, layer, depth, new_caches)
    table = w["rel_bias"][layer]
    if old_caches is None:
        oa = _mla_prompt(qcat, kcat, v, batch)
        ob = _band(table, qb, kb, vb, batch, layer)
    else:
        c_ckv, c_kr, c_k, c_v = old_caches
        oa = _mla_sample(qcat, new_caches[0], new_caches[1], c_ckv, c_kr, w, batch, layer)
        ob = _band(table, qb, kb, vb, batch, layer, c_k, c_v)
    x = _merge(x, oa, ob, w, layer)
    x = _ffn_ln(x, w["f2_w1"], w["f2_w2"], w["ln3_g"], w["ln3_b"], layer)
    return x, new_caches


def kernel(x_prompt, x_sample, cache_mla_ckv, cache_mla_krope, cache_band_k, cache_band_v, ln1_g, ln1_b, ffn1_w1, ffn1_w2, w_in, mla_q_norm_g, mla_w_uq, mla_kv_norm_g, mla_w_uk, mla_w_uv, band_rel_bias, w_proj_a, w_proj_b, w_out, ln2_g, ln2_b, ffn2_w1, ffn2_w2, ln3_g, ln3_b):
    bp, sp, d = x_prompt.shape
    bs, ts, _ = x_sample.shape
    depth = ln1_g.shape[0]
    past = cache_mla_ckv.shape[2]
    hist = cache_band_k.shape[2]
    hb = H_B * D_B
    assert d == D_MODEL and sp % MLA_QB == 0 and ts == CHUNK and hist == LEFT_CHUNKS * CHUNK
    np_, ns = bp * sp, bs * ts
    tabs_p = _rope_tables(jnp.arange(sp, dtype=jnp.int32), max(sp, min(ROW_TILE, np_)))
    tabs_s = _rope_tables(past + jnp.arange(ts, dtype=jnp.int32), min(ROW_TILE, ns))
    w = _prep_weights(ln1_g, ln1_b, ffn1_w1, ffn1_w2, w_in, mla_q_norm_g, mla_w_uq, mla_kv_norm_g,
                      mla_w_uk, mla_w_uv, band_rel_bias, w_proj_a, w_proj_b, w_out, ln2_g, ln2_b,
                      ffn2_w1, ffn2_w2, ln3_g, ln3_b)
    old = (cache_mla_ckv, cache_mla_krope,
           cache_band_k.reshape(depth, bs, hist, hb), cache_band_v.reshape(depth, bs, hist, hb))
    xp = x_prompt.reshape(np_, d)
    xs = x_sample.reshape(ns, d)
    new_p = new_s = None
    for l in range(depth):
        xp, new_p = _layer(xp, tabs_p, w, bp, l, depth, new_p)
        xs, new_s = _layer(xs, tabs_s, w, bs, l, depth, new_s, old)
    keep = min(LEFT_CHUNKS * CHUNK, sp)
    return (xp.reshape(bp, sp, d), xs.reshape(bs, ts, d),
            new_p[0].reshape(depth, bp, sp, KV_LORA), new_p[1].reshape(depth, bp, sp, ROPE_DIM),
            new_p[2].reshape(depth, bp, keep, H_B, D_B), new_p[3].reshape(depth, bp, keep, H_B, D_B),
            new_s[0].reshape(depth, bs, ts, KV_LORA), new_s[1].reshape(depth, bs, ts, ROPE_DIM),
            new_s[2].reshape(depth, bs, ts, H_B, D_B), new_s[3].reshape(depth, bs, ts, H_B, D_B))
```

```python
import functools

import jax
import jax.numpy as jnp
from jax import lax
from jax.experimental import pallas as pl
from jax.experimental.pallas import tpu as pltpu

D_MODEL = 1024
DEPTH = 4
CHUNK = 64
H_A = 8
Q_LORA = 768
KV_LORA = 256
NOPE_DIM = 64
ROPE_DIM = 32
V_DIM = 64
ROPE_BASE = 10000.0
MLA_SCALE = (NOPE_DIM + ROPE_DIM) ** -0.5
LOG2E = 1.4426950408889634
H_B = 8
D_B = 64
LEFT_CHUNKS = 8
MAX_REL = 128
BAND_SCALE = D_B ** -0.5
D_FF = 2816
ALPHA = (2 * DEPTH) ** 0.25
NORM_EPS = 1e-5
NEG_INF = -1e30

LANES = 128
MXU_WIDTH = 256
HEAD_PAD = LANES
BAND_GROUP = 4
MLA_QB = 512
ROW_TILE = 512
VMEM_LIMIT = 56 * 1024 * 1024

BF16 = jnp.bfloat16
F32 = jnp.float32


def _dot(a, b):
    return jnp.dot(a, b, preferred_element_type=F32)


def _dot_nt(a, b):
    return lax.dot_general(a, b, (((1,), (1,)), ((), ())), preferred_element_type=F32)


def _layer_norm(y, g, b):
    mu = jnp.mean(y, axis=-1, keepdims=True)
    d = y - mu
    var = jnp.mean(d * d, axis=-1, keepdims=True)
    return d * lax.rsqrt(var + NORM_EPS) * g + b


def _rms_norm(y, g):
    return y * lax.rsqrt(jnp.mean(y * y, axis=-1, keepdims=True) + NORM_EPS) * g


def _sigmoid(a):
    return 1.0 / (1.0 + jnp.exp(-a))


def _layer_spec(a, layer):
    zeros = (0,) * (a.ndim - 1)
    return pl.BlockSpec((None,) + a.shape[1:], lambda *_: (layer,) + zeros, pipeline_mode=pl.Buffered(1))


def _params(*sem):
    return pltpu.CompilerParams(dimension_semantics=sem, vmem_limit_bytes=VMEM_LIMIT)


def _ffn_chunks():
    tiles = D_FF // MXU_WIDTH
    assert tiles * MXU_WIDTH == D_FF
    first = (tiles + 1) // 2 * MXU_WIDTH
    return ((0, first), (first, D_FF - first))


def _ffn_ln_kernel(x_ref, w1_ref, w2_ref, g_ref, b_ref, o_ref):
    x = x_ref[...]
    xb = x.astype(BF16)
    acc = None
    for c0, ck in _ffn_chunks():
        a = _dot(xb, w1_ref[:, c0:c0 + ck])
        g = _dot(xb, w1_ref[:, D_FF + c0:D_FF + c0 + ck])
        h = (a * _sigmoid(a) * g).astype(BF16)
        part = _dot(h, w2_ref[c0:c0 + ck, :])
        acc = part if acc is None else acc + part
    o_ref[...] = _layer_norm(ALPHA * x + 0.5 * acc, g_ref[...], b_ref[...])


def _ffn_ln(x, w1, w2, g, b, layer):
    n, d = x.shape
    tm = min(ROW_TILE, n)
    row = pl.BlockSpec((tm, d), lambda i: (i, 0))
    return pl.pallas_call(
        _ffn_ln_kernel,
        grid=(n // tm,),
        in_specs=[row] + [_layer_spec(a, layer) for a in (w1, w2, g, b)],
        out_specs=row,
        out_shape=jax.ShapeDtypeStruct((n, d), F32),
        compiler_params=_params("parallel"),
    )(x, w1, w2, g, b)


N_PROJ_IN = 14
N_PROJ_CACHE = 4


def _proj_kernel(*refs):
    (x_ref, cq_ref, sq_ref, ckr_ref, wq_ref, wckv_ref, wkr_ref, wband_ref, qg_ref, kvg_ref,
     wqa_ref, wqr_ref, wkc_ref, wuv_ref) = refs[:N_PROJ_IN]
    (qcat_ref, kcat_ref, v_ref, qb_ref, kb_ref, vb_ref,
     ckv_ref, kr_ref, kbt_ref, vbt_ref) = refs[len(refs) - 6 - N_PROJ_CACHE:]
    xb = x_ref[...].astype(BF16)
    qn = _rms_norm(_dot(xb, wq_ref[...]), qg_ref[...]).astype(BF16)
    qa = _dot(qn, wqa_ref[...])
    qr = _dot(qn, wqr_ref[...])
    cq = cq_ref[...]
    sq = sq_ref[...]
    per_vreg = LANES // ROPE_DIM
    for h in range(H_A):
        sl = slice(h * HEAD_PAD, (h + 1) * HEAD_PAD)
        src = qr[:, (h // per_vreg) * LANES:(h // per_vreg + 1) * LANES]
        shift = (NOPE_DIM - (h % per_vreg) * ROPE_DIM) % LANES
        qs = pltpu.roll(src, shift, 1) if shift else src
        qcat_ref[:, sl] = (qa[:, sl] * cq + qs * sq).astype(BF16)
    ckv = _rms_norm(_dot(xb, wckv_ref[...]), kvg_ref[...])
    ckv_ref[...] = ckv
    t = _dot(xb, wkr_ref[...]) * ckr_ref[...]
    kr = t + pltpu.roll(t, LANES - ROPE_DIM, 1)
    kr_ref[...] = kr[:, :ROPE_DIM]
    lane = lax.broadcasted_iota(jnp.int32, kr.shape, 1)
    on_rope = (lane >= NOPE_DIM) & (lane < NOPE_DIM + ROPE_DIM)
    kr_placed = jnp.where(on_rope, pltpu.roll(kr, NOPE_DIM, 1), 0.0)
    ckvb = ckv.astype(BF16)
    kc = _dot(ckvb, wkc_ref[...])
    for h in range(H_A):
        sl = slice(h * HEAD_PAD, (h + 1) * HEAD_PAD)
        kcat_ref[:, sl] = (kc[:, sl] + kr_placed).astype(BF16)
    v_ref[...] = _dot(ckvb, wuv_ref[...]).astype(BF16)
    hb = H_B * D_B
    band = _dot(xb, wband_ref[...])
    qb_ref[...] = (band[:, :hb] * (BAND_SCALE * LOG2E)).astype(BF16)
    kb = band[:, hb:2 * hb]
    vb = band[:, 2 * hb:]
    kb_ref[...] = kb.astype(BF16)
    vb_ref[...] = vb.astype(BF16)
    kbt_ref[...] = kb
    vbt_ref[...] = vb


def _proj(x, tabs, w, batch, layer, depth, caches):
    n, d = x.shape
    tm = min(ROW_TILE, n)
    ntiles = n // tm
    s = n // batch
    keep = min(LEFT_CHUNKS * CHUNK, s)
    cq, sq, ckr = tabs
    period = cq.shape[0] // tm
    if s <= tm:
        assert keep == s
        tail_blocks, tail_idx = ntiles, (lambda i: i)
    else:
        per_seq, ntail = s // tm, keep // tm
        assert per_seq * tm == s and ntail * tm == keep
        tail_blocks = batch * ntail
        tail_idx = lambda i: (i // per_seq) * ntail + jnp.maximum(i % per_seq - (per_seq - ntail), 0)
    row = lambda c: pl.BlockSpec((tm, c), lambda i: (i, 0))
    tab = lambda c: pl.BlockSpec((tm, c), lambda i: (i % period, 0))
    stacked = lambda c: pl.BlockSpec((tm, c), lambda i: (layer * ntiles + i, 0))
    tail = lambda c: pl.BlockSpec((tm, c), lambda i: (layer * tail_blocks + tail_idx(i), 0))
    hp = H_A * HEAD_PAD
    hb = H_B * D_B
    out_shape = [
        jax.ShapeDtypeStruct((n, hp), BF16),
        jax.ShapeDtypeStruct((n, hp), BF16),
        jax.ShapeDtypeStruct((n, H_A * V_DIM), BF16),
        jax.ShapeDtypeStruct((n, hb), BF16),
        jax.ShapeDtypeStruct((n, hb), BF16),
        jax.ShapeDtypeStruct((n, hb), BF16),
        jax.ShapeDtypeStruct((depth * n, KV_LORA), F32),
        jax.ShapeDtypeStruct((depth * n, ROPE_DIM), F32),
        jax.ShapeDtypeStruct((depth * tail_blocks * tm, hb), F32),
        jax.ShapeDtypeStruct((depth * tail_blocks * tm, hb), F32),
    ]
    out_specs = [row(hp), row(hp), row(H_A * V_DIM), row(hb), row(hb), row(hb),
                 stacked(KV_LORA), stacked(ROPE_DIM), tail(hb), tail(hb)]
    weights = (w["w_q"], w["w_ckv"], w["w_kr2"], w["w_band"], w["q_g"], w["kv_g"],
               w["w_qa"], w["w_qr"], w["w_kc"], w["w_uv"])
    in_specs = ([row(d), tab(HEAD_PAD), tab(HEAD_PAD), tab(LANES)]
                + [_layer_spec(a, layer) for a in weights])
    args = [x, cq, sq, ckr, *weights]
    assert len(args) == N_PROJ_IN
    if caches is None:
        caches = tuple(jnp.zeros(o.shape, o.dtype) for o in out_shape[6:])
    in_specs += [pl.BlockSpec(memory_space=pl.ANY)] * N_PROJ_CACHE
    args += list(caches)
    aliases = {N_PROJ_IN + k: 6 + k for k in range(N_PROJ_CACHE)}
    outs = pl.pallas_call(
        _proj_kernel,
        grid=(ntiles,),
        in_specs=in_specs,
        out_specs=out_specs,
        out_shape=out_shape,
        input_output_aliases=aliases,
        compiler_params=_params("arbitrary"),
    )(*args)
    return outs[:6], tuple(outs[6:])


def _mla_prompt_kernel(q_lo_ref, q_hi_ref, k_ref, v_ref, o_ref):
    qb = MLA_QB
    rc = lax.broadcasted_iota(jnp.int32, (qb, qb), 0) // CHUNK
    cc = lax.broadcasted_iota(jnp.int32, (qb, qb), 1) // CHUNK
    diag_mask = cc <= rc
    first_half = lax.broadcasted_iota(jnp.int32, (qb, 2 * V_DIM), 1) < V_DIM

    def scores(q_ref, h, kv0):
        hs = slice(h * HEAD_PAD, (h + 1) * HEAD_PAD)
        q = q_ref[:, hs]
        s_d = jnp.where(diag_mask, _dot_nt(q, k_ref[kv0:kv0 + qb, hs]), NEG_INF)
        s_f = _dot_nt(q, k_ref[0:kv0, hs]) if kv0 else None
        return s_d, s_f

    def attend(h, kv0, s_d, s_f):
        vs = slice((h // 2) * 2 * V_DIM, (h // 2 + 1) * 2 * V_DIM)
        m = jnp.max(s_d, axis=-1, keepdims=True)
        if s_f is not None:
            m = jnp.maximum(m, jnp.max(s_f, axis=-1, keepdims=True))
        p_d = jnp.exp2(s_d - m)
        l = jnp.sum(p_d, axis=-1, keepdims=True)
        o = _dot(p_d.astype(BF16), v_ref[kv0:kv0 + qb, vs])
        if s_f is not None:
            p_f = jnp.exp2(s_f - m)
            l = l + jnp.sum(p_f, axis=-1, keepdims=True)
            o = o + _dot(p_f.astype(BF16), v_ref[0:kv0, vs])
        return o / l

    def query_block(q_ref, n):
        kv0 = n * qb
        ahead = scores(q_ref, 0, kv0)
        outs = []
        for h in range(H_A):
            cur = ahead
            if h + 1 < H_A:
                ahead = scores(q_ref, h + 1, kv0)
            outs.append(attend(h, kv0, *cur))
            if h % 2:
                hp = h // 2
                o_ref[kv0:kv0 + qb, hp * 2 * V_DIM:(hp + 1) * 2 * V_DIM] = (
                    jnp.where(first_half, outs[h - 1], outs[h]).astype(BF16))

    def variant(j):
        query_block(q_lo_ref, j)
        query_block(q_hi_ref, nq - 1 - j)

    nq = k_ref.shape[0] // qb
    for j in range(nq // 2):
        pl.when(pl.program_id(1) == j)(functools.partial(variant, j))


def _mla_prompt(qcat, kcat, v, batch):
    n = qcat.shape[0]
    s = n // batch
    nq = s // MLA_QB
    assert nq % 2 == 0
    seq = lambda c: pl.BlockSpec((s, c), lambda b, j: (b, 0))
    q_lo = pl.BlockSpec((MLA_QB, qcat.shape[1]), lambda b, j: (b * nq + j, 0))
    q_hi = pl.BlockSpec((MLA_QB, qcat.shape[1]), lambda b, j: (b * nq + nq - 1 - j, 0))
    return pl.pallas_call(
        _mla_prompt_kernel,
        grid=(batch, nq // 2),
        in_specs=[q_lo, q_hi, seq(kcat.shape[1]), seq(v.shape[1])],
        out_specs=seq(v.shape[1]),
        out_shape=jax.ShapeDtypeStruct((n, v.shape[1]), BF16),
        compiler_params=_params("parallel", "arbitrary"),
    )(qcat, qcat, kcat, v)


def _mla_sample_kernel(q_ref, ckv_ref, kr_ref, cckv_ref, ckr_ref, wabs_ref, wrope_ref, wuv_ref, o_ref):
    t = q_ref.shape[0]
    q_abs = jnp.concatenate(
        [_dot(q_ref[:, h * HEAD_PAD:(h + 1) * HEAD_PAD], wabs_ref[h]) for h in range(H_A)], axis=0).astype(BF16)
    q_rope = jnp.concatenate(
        [_dot(q_ref[:, h * HEAD_PAD:(h + 1) * HEAD_PAD], wrope_ref[h]) for h in range(H_A)], axis=0).astype(BF16)
    c_old = cckv_ref[...].astype(BF16)
    r_old = ckr_ref[...].astype(BF16)
    c_new = ckv_ref[...].astype(BF16)
    r_new = kr_ref[...].astype(BF16)
    s_old = _dot_nt(q_abs, c_old) + _dot_nt(q_rope, r_old)
    s_new = _dot_nt(q_abs, c_new) + _dot_nt(q_rope, r_new)
    m = jnp.maximum(jnp.max(s_old, axis=-1, keepdims=True), jnp.max(s_new, axis=-1, keepdims=True))
    p_old = jnp.exp2(s_old - m)
    p_new = jnp.exp2(s_new - m)
    l = jnp.sum(p_old, axis=-1, keepdims=True) + jnp.sum(p_new, axis=-1, keepdims=True)
    o_lat = ((_dot(p_old.astype(BF16), c_old) + _dot(p_new.astype(BF16), c_new)) / l).astype(BF16)
    group = lax.broadcasted_iota(jnp.int32, (t, H_A * V_DIM), 1) // V_DIM
    out = jnp.zeros((t, H_A * V_DIM), F32)
    for h in range(H_A):
        out = jnp.where(group == h, _dot(o_lat[h * t:(h + 1) * t], wuv_ref[...]), out)
    o_ref[...] = out.astype(BF16)


def _mla_sample(qcat, ckv_all, kr_all, cache_ckv, cache_kr, w, batch, layer):
    n = qcat.shape[0]
    t = n // batch
    past = cache_ckv.shape[2]
    row = lambda c: pl.BlockSpec((t, c), lambda b: (b, 0))
    new = lambda c: pl.BlockSpec((t, c), lambda b: (layer * batch + b, 0))
    old = lambda c: pl.BlockSpec((None, None, past, c), lambda b: (layer, b, 0, 0))
    weights = (w["w_abs"], w["w_ropesel"], w["w_uv"])
    return pl.pallas_call(
        _mla_sample_kernel,
        grid=(batch,),
        in_specs=[row(qcat.shape[1]), new(KV_LORA), new(ROPE_DIM), old(KV_LORA), old(ROPE_DIM)]
                 + [_layer_spec(a, layer) for a in weights],
        out_specs=row(H_A * V_DIM),
        out_shape=jax.ShapeDtypeStruct((n, H_A * V_DIM), BF16),
        compiler_params=_params("parallel"),
    )(qcat, ckv_all, kr_all, cache_ckv, cache_kr, *weights)


def _band_dims(s_len):
    group = min(BAND_GROUP, s_len // CHUNK)
    window = -(-(group + LEFT_CHUNKS) * CHUNK // LANES) * LANES
    return group, window, window - group * CHUNK


def _band_kernel(*refs, hist_rows, group, window, front):
    if hist_rows:
        tab_ref, q_ref, k_ref, v_ref, ck_ref, cv_ref, o_ref, kpad, vpad, bias = refs
    else:
        tab_ref, q_ref, k_ref, v_ref, o_ref, kpad, vpad, bias = refs
    s_len = q_ref.shape[0]
    hb = H_B * D_B
    gq = group * CHUNK
    span = (LEFT_CHUNKS + 1) * CHUNK

    @pl.when(pl.program_id(0) == 0)
    def _():
        kpad[0:front - hist_rows, :] = jnp.zeros((front - hist_rows, hb), BF16)
        vpad[0:front - hist_rows, :] = jnp.zeros((front - hist_rows, hb), BF16)
        qi = lax.broadcasted_iota(jnp.int32, (CHUNK, window), 0)
        u = lax.broadcasted_iota(jnp.int32, (CHUNK, window), 1)
        idx = jnp.clip(LEFT_CHUNKS * CHUNK + qi - u, -MAX_REL, MAX_REL) + MAX_REL
        lo = max(LEFT_CHUNKS * CHUNK - (span - 1), -MAX_REL) + MAX_REL
        for h in range(H_B):
            def fill(d, b, h=h):
                return jnp.where(idx == d, tab_ref[h, d] * LOG2E, b)
            base = lax.fori_loop(lo, 2 * MAX_REL + 1, fill, jnp.zeros((CHUNK, window), F32))
            base = jnp.where(u < span, base, NEG_INF)
            for a in range(group):
                off = front + (a - LEFT_CHUNKS) * CHUNK
                r = (h % 2) * gq + a * CHUNK
                bias[h // 2, r:r + CHUNK, :] = pltpu.roll(base, off, 1) if off else base

    if hist_rows:
        kpad[front - hist_rows:front, :] = ck_ref[...].astype(BF16)
        vpad[front - hist_rows:front, :] = cv_ref[...].astype(BF16)
    kpad[front:front + s_len, :] = k_ref[...]
    vpad[front:front + s_len, :] = v_ref[...]

    slot = lax.broadcasted_iota(jnp.int32, (1, window), 1)
    first_half = lax.broadcasted_iota(jnp.int32, (gq, 2 * D_B), 1) < D_B

    def query_group(g, _, check_exists):
        r0 = pl.multiple_of(g * gq, gq)
        q = q_ref[pl.ds(r0, gq), :]
        kw = kpad[pl.ds(r0, window), :]
        vw = vpad[pl.ds(r0, window), :]
        exists = slot >= front - hist_rows - r0
        def scores(hp):
            sl = slice(hp * 2 * D_B, (hp + 1) * 2 * D_B)
            qp = q[:, sl]
            zero = jnp.zeros_like(qp)
            q2 = jnp.concatenate([jnp.where(first_half, qp, zero), jnp.where(first_half, zero, qp)], axis=0)
            s = _dot_nt(q2, kw[:, sl]) + bias[hp]
            return jnp.where(exists, s, NEG_INF) if check_exists else s

        ahead = scores(0)
        for hp in range(H_B // 2):
            sl = slice(hp * 2 * D_B, (hp + 1) * 2 * D_B)
            s = ahead
            if hp + 1 < H_B // 2:
                ahead = scores(hp + 1)
            p = jnp.exp2(s - jnp.max(s, axis=-1, keepdims=True))
            l = jnp.sum(p, axis=-1, keepdims=True)
            o = _dot(p.astype(BF16), vw[:, sl]) / l
            o_ref[pl.ds(r0, gq), sl] = jnp.where(first_half, o[:gq], o[gq:]).astype(BF16)
        return 0

    n_groups = s_len // gq
    n_checked = min(n_groups, -(-(front - hist_rows) // gq))
    lax.fori_loop(0, n_checked, functools.partial(query_group, check_exists=True), 0)
    lax.fori_loop(n_checked, n_groups, functools.partial(query_group, check_exists=False), 0)


def _band(table, qb, kb, vb, batch, layer, cache_k=None, cache_v=None):
    n, hb = qb.shape
    s = n // batch
    hist_rows = 0 if cache_k is None else cache_k.shape[2]
    group, window, front = _band_dims(s)
    row = pl.BlockSpec((s, hb), lambda b: (b, 0))
    in_specs = [pl.BlockSpec(memory_space=pltpu.SMEM), row, row, row]
    args = [table, qb, kb, vb]
    if hist_rows:
        cache = pl.BlockSpec((None, None, hist_rows, hb), lambda b: (layer, b, 0, 0))
        in_specs += [cache, cache]
        args += [cache_k, cache_v]
    return pl.pallas_call(
        functools.partial(_band_kernel, hist_rows=hist_rows, group=group, window=window, front=front),
        grid=(batch,),
        in_specs=in_specs,
        out_specs=row,
        out_shape=jax.ShapeDtypeStruct((n, hb), BF16),
        scratch_shapes=[pltpu.VMEM((front + s, hb), BF16), pltpu.VMEM((front + s, hb), BF16),
                        pltpu.VMEM((H_B // 2, 2 * group * CHUNK, window), F32)],
        compiler_params=_params("arbitrary"),
    )(*args)


def _merge_kernel(x_ref, oa_ref, ob_ref, wga_ref, wgb_ref, wpa_ref, wpb_ref, wout_ref, g_ref, b_ref, o_ref):
    x = x_ref[...]
    xb = x.astype(BF16)
    mix = (_sigmoid(_dot(xb, wga_ref[...])) * _dot(oa_ref[...], wpa_ref[...])
           + _sigmoid(_dot(xb, wgb_ref[...])) * _dot(ob_ref[...], wpb_ref[...]))
    y = ALPHA * x + _dot(mix.astype(BF16), wout_ref[...])
    o_ref[...] = _layer_norm(y, g_ref[...], b_ref[...])


def _merge(x, oa, ob, w, layer):
    n, d = x.shape
    tm = min(ROW_TILE, n)
    row = lambda c: pl.BlockSpec((tm, c), lambda i: (i, 0))
    weights = (w["w_ga"], w["w_gb"], w["w_pa"], w["w_pb"], w["w_out"], w["ln2_g"], w["ln2_b"])
    return pl.pallas_call(
        _merge_kernel,
        grid=(n // tm,),
        in_specs=[row(d), row(oa.shape[1]), row(ob.shape[1])] + [_layer_spec(a, layer) for a in weights],
        out_specs=row(d),
        out_shape=jax.ShapeDtypeStruct((n, d), F32),
        compiler_params=_params("parallel"),
    )(x, oa, ob, *weights)


def _rope_tables(pos, rows):
    half = ROPE_DIM // 2
    inv = ROPE_BASE ** (-jnp.arange(half, dtype=F32) / half)
    ang = pos.astype(F32)[:, None] * inv[None, :]
    cos, sin = jnp.cos(ang), jnp.sin(ang)
    ck = jnp.concatenate([cos, cos], axis=-1)
    sk = jnp.concatenate([-sin, sin], axis=-1)
    t = pos.shape[0]
    pad = jnp.zeros((t, HEAD_PAD - NOPE_DIM - ROPE_DIM), F32)
    cq = jnp.concatenate([jnp.ones((t, NOPE_DIM), F32), ck, pad], axis=-1) * (MLA_SCALE * LOG2E)
    sq = jnp.concatenate([jnp.zeros((t, NOPE_DIM), F32), sk, pad], axis=-1) * (MLA_SCALE * LOG2E)
    ckr = jnp.concatenate([ck, sk, jnp.zeros((t, LANES - 2 * ROPE_DIM), F32)], axis=-1)
    rep = rows // t
    return tuple(jnp.tile(a, (rep, 1)) for a in (cq, sq, ckr))


def _swap_halves(w):
    half = w.shape[-1] // 2
    return jnp.concatenate([w[..., half:], w[..., :half]], axis=-1)


def _prep_weights(ln1_g, ln1_b, ffn1_w1, ffn1_w2, w_in, q_g, w_uq, kv_g, w_uk, w_uv, rel_bias,
                  w_pa, w_pb, w_out, ln2_g, ln2_b, ffn2_w1, ffn2_w2, ln3_g, ln3_b):
    depth = w_in.shape[0]
    hb = H_B * D_B
    c0, c1, c2 = Q_LORA, Q_LORA + KV_LORA, Q_LORA + KV_LORA + ROPE_DIM
    c3 = c2 + 3 * hb
    w_kr = w_in[:, :, c1:c2]
    qd = NOPE_DIM + ROPE_DIM
    uq = w_uq.reshape(depth, Q_LORA, H_A, qd)
    zq = jnp.zeros((depth, Q_LORA, H_A, HEAD_PAD - qd), F32)
    w_qa = jnp.concatenate([uq, zq], axis=-1).reshape(depth, Q_LORA, H_A * HEAD_PAD)
    w_qr = _swap_halves(uq[..., NOPE_DIM:]).reshape(depth, Q_LORA, H_A * ROPE_DIM)
    uk = w_uk.reshape(depth, KV_LORA, H_A, NOPE_DIM)
    w_kc = jnp.concatenate([uk, jnp.zeros((depth, KV_LORA, H_A, HEAD_PAD - NOPE_DIM), F32)],
                           axis=-1).reshape(depth, KV_LORA, H_A * HEAD_PAD)
    eye = jnp.eye(ROPE_DIM, dtype=F32)
    place = jnp.concatenate([jnp.zeros((ROPE_DIM, NOPE_DIM), F32), eye,
                             jnp.zeros((ROPE_DIM, HEAD_PAD - qd), F32)], axis=-1)
    w_abs = jnp.concatenate([jnp.transpose(uk, (0, 2, 3, 1)),
                             jnp.zeros((depth, H_A, HEAD_PAD - NOPE_DIM, KV_LORA), F32)], axis=2)
    w_ropesel = jnp.broadcast_to(place.T, (depth, H_A, HEAD_PAD, ROPE_DIM))
    bf = lambda a: a.astype(BF16)
    row = lambda a: a.reshape(depth, 1, -1)
    return dict(
        ln1_g=row(ln1_g), ln1_b=row(ln1_b), f1_w1=bf(ffn1_w1), f1_w2=bf(ffn1_w2),
        w_q=bf(w_in[:, :, :c0]), w_ckv=bf(w_in[:, :, c0:c1]),
        w_kr2=bf(jnp.concatenate([w_kr, _swap_halves(w_kr),
                                  jnp.zeros((depth, D_MODEL, LANES - 2 * ROPE_DIM), F32)], axis=-1)),
        w_band=bf(w_in[:, :, c2:c3]), w_ga=bf(w_in[:, :, c3:c3 + D_MODEL]), w_gb=bf(w_in[:, :, c3 + D_MODEL:]),
        q_g=row(q_g), kv_g=row(kv_g), w_qa=bf(w_qa), w_qr=bf(w_qr), w_kc=bf(w_kc),
        w_uv=bf(w_uv), w_abs=bf(w_abs), w_ropesel=bf(w_ropesel), rel_bias=rel_bias,
        w_pa=bf(w_pa), w_pb=bf(w_pb), w_out=bf(w_out), ln2_g=row(ln2_g), ln2_b=row(ln2_b),
        f2_w1=bf(ffn2_w1), f2_w2=bf(ffn2_w2), ln3_g=row(ln3_g), ln3_b=row(ln3_b),
    )


def _layer(x, tabs, w, batch, layer, depth, new_caches, old_caches=None):
    x = _ffn_ln(x, w["f1_w1"], w["f1_w2"], w["ln1_g"], w["ln1_b"], layer)
    (qcat, kcat, v, qb, kb, vb), new_caches = _proj(x, tabs, w, batch, layer, depth, new_caches)
    table = w["rel_bias"][layer]
    if old_caches is None:
        oa = _mla_prompt(qcat, kcat, v, batch)
        ob = _band(table, qb, kb, vb, batch, layer)
    else:
        c_ckv, c_kr, c_k, c_v = old_caches
        oa = _mla_sample(qcat, new_caches[0], new_caches[1], c_ckv, c_kr, w, batch, layer)
        ob = _band(table, qb, kb, vb, batch, layer, c_k, c_v)
    x = _merge(x, oa, ob, w, layer)
    x = _ffn_ln(x, w["f2_w1"], w["f2_w2"], w["ln3_g"], w["ln3_b"], layer)
    return x, new_caches


def kernel(x_prompt, x_sample, cache_mla_ckv, cache_mla_krope, cache_band_k, cache_band_v, ln1_g, ln1_b, ffn1_w1, ffn1_w2, w_in, mla_q_norm_g, mla_w_uq, mla_kv_norm_g, mla_w_uk, mla_w_uv, band_rel_bias, w_proj_a, w_proj_b, w_out, ln2_g, ln2_b, ffn2_w1, ffn2_w2, ln3_g, ln3_b):
    bp, sp, d = x_prompt.shape
    bs, ts, _ = x_sample.shape
    depth = ln1_g.shape[0]
    past = cache_mla_ckv.shape[2]
    hist = cache_band_k.shape[2]
    hb = H_B * D_B
    assert d == D_MODEL and sp % MLA_QB == 0 and ts == CHUNK and hist == LEFT_CHUNKS * CHUNK
    np_, ns = bp * sp, bs * ts
    tabs_p = _rope_tables(jnp.arange(sp, dtype=jnp.int32), max(sp, min(ROW_TILE, np_)))
    tabs_s = _rope_tables(past + jnp.arange(ts, dtype=jnp.int32), min(ROW_TILE, ns))
    w = _prep_weights(ln1_g, ln1_b, ffn1_w1, ffn1_w2, w_in, mla_q_norm_g, mla_w_uq, mla_kv_norm_g,
                      mla_w_uk, mla_w_uv, band_rel_bias, w_proj_a, w_proj_b, w_out, ln2_g, ln2_b,
                      ffn2_w1, ffn2_w2, ln3_g, ln3_b)
    old = (cache_mla_ckv, cache_mla_krope,
           cache_band_k.reshape(depth, bs, hist, hb), cache_band_v.reshape(depth, bs, hist, hb))
    xp = x_prompt.reshape(np_, d)
    xs = x_sample.reshape(ns, d)
    new_p = new_s = None
    for l in range(depth):
        xp, new_p = _layer(xp, tabs_p, w, bp, l, depth, new_p)
        xs, new_s = _layer(xs, tabs_s, w, bs, l, depth, new_s, old)
    keep = min(LEFT_CHUNKS * CHUNK, sp)
    return (xp.reshape(bp, sp, d), xs.reshape(bs, ts, d),
            new_p[0].reshape(depth, bp, sp, KV_LORA), new_p[1].reshape(depth, bp, sp, ROPE_DIM),
            new_p[2].reshape(depth, bp, keep, H_B, D_B), new_p[3].reshape(depth, bp, keep, H_B, D_B),
            new_s[0].reshape(depth, bs, ts, KV_LORA), new_s[1].reshape(depth, bs, ts, ROPE_DIM),
            new_s[2].reshape(depth, bs, ts, H_B, D_B), new_s[3].reshape(depth, bs, ts, H_B, D_B))
```

```python
import functools

import jax
import jax.numpy as jnp
from jax import lax
from jax.experimental import pallas as pl
from jax.experimental.pallas import tpu as pltpu

D_MODEL = 1024
DEPTH = 4
CHUNK = 64
H_A = 8
Q_LORA = 768
KV_LORA = 256
NOPE_DIM = 64
ROPE_DIM = 32
V_DIM = 64
ROPE_BASE = 10000.0
MLA_SCALE = (NOPE_DIM + ROPE_DIM) ** -0.5
LOG2E = 1.4426950408889634
H_B = 8
D_B = 64
LEFT_CHUNKS = 8
MAX_REL = 128
BAND_SCALE = D_B ** -0.5
D_FF = 2816
ALPHA = (2 * DEPTH) ** 0.25
NORM_EPS = 1e-5
NEG_INF = -1e30

LANES = 128
MXU_WIDTH = 256
HEAD_PAD = LANES
BAND_GROUP = 4
MLA_QB = 512
ROW_TILE = 512
FFN_ROW_TILE = 1024
FFN_CHUNK_TILES = 4
VMEM_LIMIT = 56 * 1024 * 1024

BF16 = jnp.bfloat16
F32 = jnp.float32


def _dot(a, b):
    return jnp.dot(a, b, preferred_element_type=F32)


def _dot_nt(a, b):
    return lax.dot_general(a, b, (((1,), (1,)), ((), ())), preferred_element_type=F32)


def _layer_norm(y, g, b):
    mu = jnp.mean(y, axis=-1, keepdims=True)
    d = y - mu
    var = jnp.mean(d * d, axis=-1, keepdims=True)
    return d * lax.rsqrt(var + NORM_EPS) * g + b


def _rms_norm(y, g):
    return y * lax.rsqrt(jnp.mean(y * y, axis=-1, keepdims=True) + NORM_EPS) * g


def _sigmoid(a):
    return 1.0 / (1.0 + jnp.exp(-a))


def _layer_spec(a, layer):
    zeros = (0,) * (a.ndim - 1)
    return pl.BlockSpec((None,) + a.shape[1:], lambda *_: (layer,) + zeros, pipeline_mode=pl.Buffered(1))


def _params(*sem):
    return pltpu.CompilerParams(dimension_semantics=sem, vmem_limit_bytes=VMEM_LIMIT)


def _ffn_chunks():
    tiles = D_FF // MXU_WIDTH
    assert tiles * MXU_WIDTH == D_FF
    bounds = list(range(0, tiles, FFN_CHUNK_TILES)) + [tiles]
    return tuple((lo * MXU_WIDTH, (hi - lo) * MXU_WIDTH) for lo, hi in zip(bounds, bounds[1:]))


def _ffn_ln_kernel(x_ref, w1_ref, w2_ref, g_ref, b_ref, o_ref):
    x = x_ref[...]
    xb = x.astype(BF16)
    acc = None
    for c0, ck in _ffn_chunks():
        a = _dot(xb, w1_ref[:, c0:c0 + ck])
        g = _dot(xb, w1_ref[:, D_FF + c0:D_FF + c0 + ck])
        h = (a * _sigmoid(a) * g).astype(BF16)
        part = _dot(h, w2_ref[c0:c0 + ck, :])
        acc = part if acc is None else acc + part
    o_ref[...] = _layer_norm(ALPHA * x + 0.5 * acc, g_ref[...], b_ref[...])


def _ffn_ln(x, w1, w2, g, b, layer):
    n, d = x.shape
    tm = min(FFN_ROW_TILE, n)
    row = pl.BlockSpec((tm, d), lambda i: (i, 0))
    return pl.pallas_call(
        _ffn_ln_kernel,
        grid=(n // tm,),
        in_specs=[row] + [_layer_spec(a, layer) for a in (w1, w2, g, b)],
        out_specs=row,
        out_shape=jax.ShapeDtypeStruct((n, d), F32),
        compiler_params=_params("parallel"),
    )(x, w1, w2, g, b)


N_PROJ_IN = 14
N_PROJ_CACHE = 4


def _proj_kernel(*refs):
    (x_ref, cq_ref, sq_ref, ckr_ref, wq_ref, wckv_ref, wkr_ref, wband_ref, qg_ref, kvg_ref,
     wqa_ref, wqr_ref, wkc_ref, wuv_ref) = refs[:N_PROJ_IN]
    (qcat_ref, kcat_ref, v_ref, qb_ref, kb_ref, vb_ref,
     ckv_ref, kr_ref, kbt_ref, vbt_ref) = refs[len(refs) - 6 - N_PROJ_CACHE:]
    xb = x_ref[...].astype(BF16)
    qn = _rms_norm(_dot(xb, wq_ref[...]), qg_ref[...]).astype(BF16)
    qa = _dot(qn, wqa_ref[...])
    qr = _dot(qn, wqr_ref[...])
    cq = cq_ref[...]
    sq = sq_ref[...]
    per_vreg = LANES // ROPE_DIM
    for h in range(H_A):
        sl = slice(h * HEAD_PAD, (h + 1) * HEAD_PAD)
        src = qr[:, (h // per_vreg) * LANES:(h // per_vreg + 1) * LANES]
        shift = (NOPE_DIM - (h % per_vreg) * ROPE_DIM) % LANES
        qs = pltpu.roll(src, shift, 1) if shift else src
        qcat_ref[:, sl] = (qa[:, sl] * cq + qs * sq).astype(BF16)
    ckv = _rms_norm(_dot(xb, wckv_ref[...]), kvg_ref[...])
    ckv_ref[...] = ckv
    t = _dot(xb, wkr_ref[...]) * ckr_ref[...]
    kr = t + pltpu.roll(t, LANES - ROPE_DIM, 1)
    kr_ref[...] = kr[:, :ROPE_DIM]
    lane = lax.broadcasted_iota(jnp.int32, kr.shape, 1)
    on_rope = (lane >= NOPE_DIM) & (lane < NOPE_DIM + ROPE_DIM)
    kr_placed = jnp.where(on_rope, pltpu.roll(kr, NOPE_DIM, 1), 0.0)
    ckvb = ckv.astype(BF16)
    kc = _dot(ckvb, wkc_ref[...])
    for h in range(H_A):
        sl = slice(h * HEAD_PAD, (h + 1) * HEAD_PAD)
        kcat_ref[:, sl] = (kc[:, sl] + kr_placed).astype(BF16)
    v_ref[...] = _dot(ckvb, wuv_ref[...]).astype(BF16)
    hb = H_B * D_B
    band = _dot(xb, wband_ref[...])
    qb_ref[...] = (band[:, :hb] * (BAND_SCALE * LOG2E)).astype(BF16)
    kb = band[:, hb:2 * hb]
    vb = band[:, 2 * hb:]
    kb_ref[...] = kb.astype(BF16)
    vb_ref[...] = vb.astype(BF16)
    kbt_ref[...] = kb
    vbt_ref[...] = vb


def _proj(x, tabs, w, batch, layer, depth, caches):
    n, d = x.shape
    tm = min(ROW_TILE, n)
    ntiles = n // tm
    s = n // batch
    keep = min(LEFT_CHUNKS * CHUNK, s)
    cq, sq, ckr = tabs
    period = cq.shape[0] // tm
    if s <= tm:
        assert keep == s
        tail_blocks, tail_idx = ntiles, (lambda i: i)
    else:
        per_seq, ntail = s // tm, keep // tm
        assert per_seq * tm == s and ntail * tm == keep
        tail_blocks = batch * ntail
        tail_idx = lambda i: (i // per_seq) * ntail + jnp.maximum(i % per_seq - (per_seq - ntail), 0)
    row = lambda c: pl.BlockSpec((tm, c), lambda i: (i, 0))
    tab = lambda c: pl.BlockSpec((tm, c), lambda i: (i % period, 0))
    stacked = lambda c: pl.BlockSpec((tm, c), lambda i: (layer * ntiles + i, 0))
    tail = lambda c: pl.BlockSpec((tm, c), lambda i: (layer * tail_blocks + tail_idx(i), 0))
    hp = H_A * HEAD_PAD
    hb = H_B * D_B
    out_shape = [
        jax.ShapeDtypeStruct((n, hp), BF16),
        jax.ShapeDtypeStruct((n, hp), BF16),
        jax.ShapeDtypeStruct((n, H_A * V_DIM), BF16),
        jax.ShapeDtypeStruct((n, hb), BF16),
        jax.ShapeDtypeStruct((n, hb), BF16),
        jax.ShapeDtypeStruct((n, hb), BF16),
        jax.ShapeDtypeStruct((depth * n, KV_LORA), F32),
        jax.ShapeDtypeStruct((depth * n, ROPE_DIM), F32),
        jax.ShapeDtypeStruct((depth * tail_blocks * tm, hb), F32),
        jax.ShapeDtypeStruct((depth * tail_blocks * tm, hb), F32),
    ]
    out_specs = [row(hp), row(hp), row(H_A * V_DIM), row(hb), row(hb), row(hb),
                 stacked(KV_LORA), stacked(ROPE_DIM), tail(hb), tail(hb)]
    weights = (w["w_q"], w["w_ckv"], w["w_kr2"], w["w_band"], w["q_g"], w["kv_g"],
               w["w_qa"], w["w_qr"], w["w_kc"], w["w_uv"])
    in_specs = ([row(d), tab(HEAD_PAD), tab(HEAD_PAD), tab(LANES)]
                + [_layer_spec(a, layer) for a in weights])
    args = [x, cq, sq, ckr, *weights]
    assert len(args) == N_PROJ_IN
    if caches is None:
        caches = tuple(jnp.zeros(o.shape, o.dtype) for o in out_shape[6:])
    in_specs += [pl.BlockSpec(memory_space=pl.ANY)] * N_PROJ_CACHE
    args += list(caches)
    aliases = {N_PROJ_IN + k: 6 + k for k in range(N_PROJ_CACHE)}
    outs = pl.pallas_call(
        _proj_kernel,
        grid=(ntiles,),
        in_specs=in_specs,
        out_specs=out_specs,
        out_shape=out_shape,
        input_output_aliases=aliases,
        compiler_params=_params("arbitrary"),
    )(*args)
    return outs[:6], tuple(outs[6:])


def _mla_prompt_kernel(q_lo_ref, q_hi_ref, k_ref, v_ref, o_ref):
    qb = MLA_QB
    rc = lax.broadcasted_iota(jnp.int32, (qb, qb), 0) // CHUNK
    cc = lax.broadcasted_iota(jnp.int32, (qb, qb), 1) // CHUNK
    diag_mask = cc <= rc
    first_half = lax.broadcasted_iota(jnp.int32, (qb, 2 * V_DIM), 1) < V_DIM

    def scores(q_ref, h, kv0):
        hs = slice(h * HEAD_PAD, (h + 1) * HEAD_PAD)
        q = q_ref[:, hs]
        s_d = jnp.where(diag_mask, _dot_nt(q, k_ref[kv0:kv0 + qb, hs]), NEG_INF)
        s_f = _dot_nt(q, k_ref[0:kv0, hs]) if kv0 else None
        return s_d, s_f

    def attend(h, kv0, s_d, s_f):
        vs = slice((h // 2) * 2 * V_DIM, (h // 2 + 1) * 2 * V_DIM)
        m = jnp.max(s_d, axis=-1, keepdims=True)
        if s_f is not None:
            m = jnp.maximum(m, jnp.max(s_f, axis=-1, keepdims=True))
        p_d = jnp.exp2(s_d - m)
        l = jnp.sum(p_d, axis=-1, keepdims=True)
        o = _dot(p_d.astype(BF16), v_ref[kv0:kv0 + qb, vs])
        if s_f is not None:
            p_f = jnp.exp2(s_f - m)
            l = l + jnp.sum(p_f, axis=-1, keepdims=True)
            o = o + _dot(p_f.astype(BF16), v_ref[0:kv0, vs])
        return o / l

    def query_block(q_ref, n):
        kv0 = n * qb
        ahead = scores(q_ref, 0, kv0)
        outs = []
        for h in range(H_A):
            cur = ahead
            if h + 1 < H_A:
                ahead = scores(q_ref, h + 1, kv0)
            outs.append(attend(h, kv0, *cur))
            if h % 2:
                hp = h // 2
                o_ref[kv0:kv0 + qb, hp * 2 * V_DIM:(hp + 1) * 2 * V_DIM] = (
                    jnp.where(first_half, outs[h - 1], outs[h]).astype(BF16))

    def variant(j):
        query_block(q_lo_ref, j)
        query_block(q_hi_ref, nq - 1 - j)

    nq = k_ref.shape[0] // qb
    for j in range(nq // 2):
        pl.when(pl.program_id(1) == j)(functools.partial(variant, j))


def _mla_prompt(qcat, kcat, v, batch):
    n = qcat.shape[0]
    s = n // batch
    nq = s // MLA_QB
    assert nq % 2 == 0
    seq = lambda c: pl.BlockSpec((s, c), lambda b, j: (b, 0))
    q_lo = pl.BlockSpec((MLA_QB, qcat.shape[1]), lambda b, j: (b * nq + j, 0))
    q_hi = pl.BlockSpec((MLA_QB, qcat.shape[1]), lambda b, j: (b * nq + nq - 1 - j, 0))
    return pl.pallas_call(
        _mla_prompt_kernel,
        grid=(batch, nq // 2),
        in_specs=[q_lo, q_hi, seq(kcat.shape[1]), seq(v.shape[1])],
        out_specs=seq(v.shape[1]),
        out_shape=jax.ShapeDtypeStruct((n, v.shape[1]), BF16),
        compiler_params=_params("parallel", "arbitrary"),
    )(qcat, qcat, kcat, v)


def _mla_sample_kernel(q_ref, ckv_ref, kr_ref, cckv_ref, ckr_ref, wabs_ref, wrope_ref, wuv_ref, o_ref):
    t = q_ref.shape[0]
    q_abs = jnp.concatenate(
        [_dot(q_ref[:, h * HEAD_PAD:(h + 1) * HEAD_PAD], wabs_ref[h]) for h in range(H_A)], axis=0).astype(BF16)
    q_rope = jnp.concatenate(
        [_dot(q_ref[:, h * HEAD_PAD:(h + 1) * HEAD_PAD], wrope_ref[h]) for h in range(H_A)], axis=0).astype(BF16)
    c_old = cckv_ref[...].astype(BF16)
    r_old = ckr_ref[...].astype(BF16)
    c_new = ckv_ref[...].astype(BF16)
    r_new = kr_ref[...].astype(BF16)
    s_old = _dot_nt(q_abs, c_old) + _dot_nt(q_rope, r_old)
    s_new = _dot_nt(q_abs, c_new) + _dot_nt(q_rope, r_new)
    m = jnp.maximum(jnp.max(s_old, axis=-1, keepdims=True), jnp.max(s_new, axis=-1, keepdims=True))
    p_old = jnp.exp2(s_old - m)
    p_new = jnp.exp2(s_new - m)
    l = jnp.sum(p_old, axis=-1, keepdims=True) + jnp.sum(p_new, axis=-1, keepdims=True)
    o_lat = ((_dot(p_old.astype(BF16), c_old) + _dot(p_new.astype(BF16), c_new)) / l).astype(BF16)
    group = lax.broadcasted_iota(jnp.int32, (t, H_A * V_DIM), 1) // V_DIM
    out = jnp.zeros((t, H_A * V_DIM), F32)
    for h in range(H_A):
        out = jnp.where(group == h, _dot(o_lat[h * t:(h + 1) * t], wuv_ref[...]), out)
    o_ref[...] = out.astype(BF16)


def _mla_sample(qcat, ckv_all, kr_all, cache_ckv, cache_kr, w, batch, layer):
    n = qcat.shape[0]
    t = n // batch
    past = cache_ckv.shape[2]
    row = lambda c: pl.BlockSpec((t, c), lambda b: (b, 0))
    new = lambda c: pl.BlockSpec((t, c), lambda b: (layer * batch + b, 0))
    old = lambda c: pl.BlockSpec((None, None, past, c), lambda b: (layer, b, 0, 0))
    weights = (w["w_abs"], w["w_ropesel"], w["w_uv"])
    return pl.pallas_call(
        _mla_sample_kernel,
        grid=(batch,),
        in_specs=[row(qcat.shape[1]), new(KV_LORA), new(ROPE_DIM), old(KV_LORA), old(ROPE_DIM)]
                 + [_layer_spec(a, layer) for a in weights],
        out_specs=row(H_A * V_DIM),
        out_shape=jax.ShapeDtypeStruct((n, H_A * V_DIM), BF16),
        compiler_params=_params("parallel"),
    )(qcat, ckv_all, kr_all, cache_ckv, cache_kr, *weights)


def _band_dims(s_len):
    group = min(BAND_GROUP, s_len // CHUNK)
    window = -(-(group + LEFT_CHUNKS) * CHUNK // LANES) * LANES
    return group, window, window - group * CHUNK


def _band_kernel(*refs, hist_rows, group, window, front):
    if hist_rows:
        tab_ref, q_ref, k_ref, v_ref, ck_ref, cv_ref, o_ref, kpad, vpad, bias = refs
    else:
        tab_ref, q_ref, k_ref, v_ref, o_ref, kpad, vpad, bias = refs
    s_len = q_ref.shape[0]
    hb = H_B * D_B
    gq = group * CHUNK
    span = (LEFT_CHUNKS + 1) * CHUNK

    @pl.when(pl.program_id(0) == 0)
    def _():
        kpad[0:front - hist_rows, :] = jnp.zeros((front - hist_rows, hb), BF16)
        vpad[0:front - hist_rows, :] = jnp.zeros((front - hist_rows, hb), BF16)
        qi = lax.broadcasted_iota(jnp.int32, (CHUNK, window), 0)
        u = lax.broadcasted_iota(jnp.int32, (CHUNK, window), 1)
        idx = jnp.clip(LEFT_CHUNKS * CHUNK + qi - u, -MAX_REL, MAX_REL) + MAX_REL
        lo = max(LEFT_CHUNKS * CHUNK - (span - 1), -MAX_REL) + MAX_REL
        for h in range(H_B):
            def fill(d, b, h=h):
                return jnp.where(idx == d, tab_ref[h, d] * LOG2E, b)
            base = lax.fori_loop(lo, 2 * MAX_REL + 1, fill, jnp.zeros((CHUNK, window), F32))
            base = jnp.where(u < span, base, NEG_INF)
            for a in range(group):
                off = front + (a - LEFT_CHUNKS) * CHUNK
                r = (h % 2) * gq + a * CHUNK
                bias[h // 2, r:r + CHUNK, :] = pltpu.roll(base, off, 1) if off else base

    if hist_rows:
        kpad[front - hist_rows:front, :] = ck_ref[...].astype(BF16)
        vpad[front - hist_rows:front, :] = cv_ref[...].astype(BF16)
    kpad[front:front + s_len, :] = k_ref[...]
    vpad[front:front + s_len, :] = v_ref[...]

    slot = lax.broadcasted_iota(jnp.int32, (1, window), 1)
    first_half = lax.broadcasted_iota(jnp.int32, (gq, 2 * D_B), 1) < D_B

    def query_group(g, _, check_exists):
        r0 = pl.multiple_of(g * gq, gq)
        q = q_ref[pl.ds(r0, gq), :]
        kw = kpad[pl.ds(r0, window), :]
        vw = vpad[pl.ds(r0, window), :]
        exists = slot >= front - hist_rows - r0
        def scores(hp):
            sl = slice(hp * 2 * D_B, (hp + 1) * 2 * D_B)
            qp = q[:, sl]
            zero = jnp.zeros_like(qp)
            q2 = jnp.concatenate([jnp.where(first_half, qp, zero), jnp.where(first_half, zero, qp)], axis=0)
            s = _dot_nt(q2, kw[:, sl]) + bias[hp]
            return jnp.where(exists, s, NEG_INF) if check_exists else s

        ahead = scores(0)
        for hp in range(H_B // 2):
            sl = slice(hp * 2 * D_B, (hp + 1) * 2 * D_B)
            s = ahead
            if hp + 1 < H_B // 2:
                ahead = scores(hp + 1)
            p = jnp.exp2(s - jnp.max(s, axis=-1, keepdims=True))
            l = jnp.sum(p, axis=-1, keepdims=True)
            o = _dot(p.astype(BF16), vw[:, sl]) / l
            o_ref[pl.ds(r0, gq), sl] = jnp.where(first_half, o[:gq], o[gq:]).astype(BF16)
        return 0

    n_groups = s_len // gq
    n_checked = min(n_groups, -(-(front - hist_rows) // gq))
    lax.fori_loop(0, n_checked, functools.partial(query_group, check_exists=True), 0)
    lax.fori_loop(n_checked, n_groups, functools.partial(query_group, check_exists=False), 0)


def _band(table, qb, kb, vb, batch, layer, cache_k=None, cache_v=None):
    n, hb = qb.shape
    s = n // batch
    hist_rows = 0 if cache_k is None else cache_k.shape[2]
    group, window, front = _band_dims(s)
    row = pl.BlockSpec((s, hb), lambda b: (b, 0))
    in_specs = [pl.BlockSpec(memory_space=pltpu.SMEM), row, row, row]
    args = [table, qb, kb, vb]
    if hist_rows:
        cache = pl.BlockSpec((None, None, hist_rows, hb), lambda b: (layer, b, 0, 0))
        in_specs += [cache, cache]
        args += [cache_k, cache_v]
    return pl.pallas_call(
        functools.partial(_band_kernel, hist_rows=hist_rows, group=group, window=window, front=front),
        grid=(batch,),
        in_specs=in_specs,
        out_specs=row,
        out_shape=jax.ShapeDtypeStruct((n, hb), BF16),
        scratch_shapes=[pltpu.VMEM((front + s, hb), BF16), pltpu.VMEM((front + s, hb), BF16),
                        pltpu.VMEM((H_B // 2, 2 * group * CHUNK, window), F32)],
        compiler_params=_params("arbitrary"),
    )(*args)


def _merge_kernel(x_ref, oa_ref, ob_ref, wga_ref, wgb_ref, wpa_ref, wpb_ref, wout_ref, g_ref, b_ref, o_ref):
    half = x_ref.shape[0] // 2
    for r in range(2):
        rows = slice(r * half, (r + 1) * half)
        x = x_ref[rows, :]
        xb = x.astype(BF16)
        mix = (_sigmoid(_dot(xb, wga_ref[...])) * _dot(oa_ref[rows, :], wpa_ref[...])
               + _sigmoid(_dot(xb, wgb_ref[...])) * _dot(ob_ref[rows, :], wpb_ref[...]))
        y = ALPHA * x + _dot(mix.astype(BF16), wout_ref[...])
        o_ref[rows, :] = _layer_norm(y, g_ref[...], b_ref[...])


def _merge(x, oa, ob, w, layer):
    n, d = x.shape
    tm = min(ROW_TILE, n)
    row = lambda c: pl.BlockSpec((tm, c), lambda i: (i, 0))
    weights = (w["w_ga"], w["w_gb"], w["w_pa"], w["w_pb"], w["w_out"], w["ln2_g"], w["ln2_b"])
    return pl.pallas_call(
        _merge_kernel,
        grid=(n // tm,),
        in_specs=[row(d), row(oa.shape[1]), row(ob.shape[1])] + [_layer_spec(a, layer) for a in weights],
        out_specs=row(d),
        out_shape=jax.ShapeDtypeStruct((n, d), F32),
        compiler_params=_params("parallel"),
    )(x, oa, ob, *weights)


def _rope_tables(pos, rows):
    half = ROPE_DIM // 2
    inv = ROPE_BASE ** (-jnp.arange(half, dtype=F32) / half)
    ang = pos.astype(F32)[:, None] * inv[None, :]
    cos, sin = jnp.cos(ang), jnp.sin(ang)
    ck = jnp.concatenate([cos, cos], axis=-1)
    sk = jnp.concatenate([-sin, sin], axis=-1)
    t = pos.shape[0]
    pad = jnp.zeros((t, HEAD_PAD - NOPE_DIM - ROPE_DIM), F32)
    cq = jnp.concatenate([jnp.ones((t, NOPE_DIM), F32), ck, pad], axis=-1) * (MLA_SCALE * LOG2E)
    sq = jnp.concatenate([jnp.zeros((t, NOPE_DIM), F32), sk, pad], axis=-1) * (MLA_SCALE * LOG2E)
    ckr = jnp.concatenate([ck, sk, jnp.zeros((t, LANES - 2 * ROPE_DIM), F32)], axis=-1)
    rep = rows // t
    return tuple(jnp.tile(a, (rep, 1)) for a in (cq, sq, ckr))


def _swap_halves(w):
    half = w.shape[-1] // 2
    return jnp.concatenate([w[..., half:], w[..., :half]], axis=-1)


def _prep_weights(ln1_g, ln1_b, ffn1_w1, ffn1_w2, w_in, q_g, w_uq, kv_g, w_uk, w_uv, rel_bias,
                  w_pa, w_pb, w_out, ln2_g, ln2_b, ffn2_w1, ffn2_w2, ln3_g, ln3_b):
    depth = w_in.shape[0]
    hb = H_B * D_B
    c0, c1, c2 = Q_LORA, Q_LORA + KV_LORA, Q_LORA + KV_LORA + ROPE_DIM
    c3 = c2 + 3 * hb
    w_kr = w_in[:, :, c1:c2]
    qd = NOPE_DIM + ROPE_DIM
    uq = w_uq.reshape(depth, Q_LORA, H_A, qd)
    zq = jnp.zeros((depth, Q_LORA, H_A, HEAD_PAD - qd), F32)
    w_qa = jnp.concatenate([uq, zq], axis=-1).reshape(depth, Q_LORA, H_A * HEAD_PAD)
    w_qr = _swap_halves(uq[..., NOPE_DIM:]).reshape(depth, Q_LORA, H_A * ROPE_DIM)
    uk = w_uk.reshape(depth, KV_LORA, H_A, NOPE_DIM)
    w_kc = jnp.concatenate([uk, jnp.zeros((depth, KV_LORA, H_A, HEAD_PAD - NOPE_DIM), F32)],
                           axis=-1).reshape(depth, KV_LORA, H_A * HEAD_PAD)
    eye = jnp.eye(ROPE_DIM, dtype=F32)
    place = jnp.concatenate([jnp.zeros((ROPE_DIM, NOPE_DIM), F32), eye,
                             jnp.zeros((ROPE_DIM, HEAD_PAD - qd), F32)], axis=-1)
    w_abs = jnp.concatenate([jnp.transpose(uk, (0, 2, 3, 1)),
                             jnp.zeros((depth, H_A, HEAD_PAD - NOPE_DIM, KV_LORA), F32)], axis=2)
    w_ropesel = jnp.broadcast_to(place.T, (depth, H_A, HEAD_PAD, ROPE_DIM))
    bf = lambda a: a.astype(BF16)
    row = lambda a: a.reshape(depth, 1, -1)
    return dict(
        ln1_g=row(ln1_g), ln1_b=row(ln1_b), f1_w1=bf(ffn1_w1), f1_w2=bf(ffn1_w2),
        w_q=bf(w_in[:, :, :c0]), w_ckv=bf(w_in[:, :, c0:c1]),
        w_kr2=bf(jnp.concatenate([w_kr, _swap_halves(w_kr),
                                  jnp.zeros((depth, D_MODEL, LANES - 2 * ROPE_DIM), F32)], axis=-1)),
        w_band=bf(w_in[:, :, c2:c3]), w_ga=bf(w_in[:, :, c3:c3 + D_MODEL]), w_gb=bf(w_in[:, :, c3 + D_MODEL:]),
        q_g=row(q_g), kv_g=row(kv_g), w_qa=bf(w_qa), w_qr=bf(w_qr), w_kc=bf(w_kc),
        w_uv=bf(w_uv), w_abs=bf(w_abs), w_ropesel=bf(w_ropesel), rel_bias=rel_bias,
        w_pa=bf(w_pa), w_pb=bf(w_pb), w_out=bf(w_out), ln2_g=row(ln2_g), ln2_b=row(ln2_b),
        f2_w1=bf(ffn2_w1), f2_w2=bf(ffn2_w2), ln3_g=row(ln3_g), ln3_b=row(ln3_b),
    )


def _layer(x, tabs, w, batch, layer, depth, new_caches, old_caches=None):
    x = _ffn_ln(x, w["f1_w1"], w["f1_w2"], w["ln1_g"], w["ln1_b"], layer)
    (qcat, kcat, v, qb, kb, vb), new_caches = _proj(x, tabs, w, batch, layer, depth, new_caches)
    table = w["rel_bias"][layer]
    if old_caches is None:
        oa = _mla_prompt(qcat, kcat, v, batch)
        ob = _band(table, qb, kb, vb, batch, layer)
    else:
        c_ckv, c_kr, c_k, c_v = old_caches
        oa = _mla_sample(qcat, new_caches[0], new_caches[1], c_ckv, c_kr, w, batch, layer)
        ob = _band(table, qb, kb, vb, batch, layer, c_k, c_v)
    x = _merge(x, oa, ob, w, layer)
    x = _ffn_ln(x, w["f2_w1"], w["f2_w2"], w["ln3_g"], w["ln3_b"], layer)
    return x, new_caches


def kernel(x_prompt, x_sample, cache_mla_ckv, cache_mla_krope, cache_band_k, cache_band_v, ln1_g, ln1_b, ffn1_w1, ffn1_w2, w_in, mla_q_norm_g, mla_w_uq, mla_kv_norm_g, mla_w_uk, mla_w_uv, band_rel_bias, w_proj_a, w_proj_b, w_out, ln2_g, ln2_b, ffn2_w1, ffn2_w2, ln3_g, ln3_b):
    bp, sp, d = x_prompt.shape
    bs, ts, _ = x_sample.shape
    depth = ln1_g.shape[0]
    past = cache_mla_ckv.shape[2]
    hist = cache_band_k.shape[2]
    hb = H_B * D_B
    assert d == D_MODEL and sp % MLA_QB == 0 and ts == CHUNK and hist == LEFT_CHUNKS * CHUNK
    np_, ns = bp * sp, bs * ts
    tabs_p = _rope_tables(jnp.arange(sp, dtype=jnp.int32), max(sp, min(ROW_TILE, np_)))
    tabs_s = _rope_tables(past + jnp.arange(ts, dtype=jnp.int32), min(ROW_TILE, ns))
    w = _prep_weights(ln1_g, ln1_b, ffn1_w1, ffn1_w2, w_in, mla_q_norm_g, mla_w_uq, mla_kv_norm_g,
                      mla_w_uk, mla_w_uv, band_rel_bias, w_proj_a, w_proj_b, w_out, ln2_g, ln2_b,
                      ffn2_w1, ffn2_w2, ln3_g, ln3_b)
    old = (cache_mla_ckv, cache_mla_krope,
           cache_band_k.reshape(depth, bs, hist, hb), cache_band_v.reshape(depth, bs, hist, hb))
    xp = x_prompt.reshape(np_, d)
    xs = x_sample.reshape(ns, d)
    new_p = new_s = None
    for l in range(depth):
        xp, new_p = _layer(xp, tabs_p, w, bp, l, depth, new_p)
        xs, new_s = _layer(xs, tabs_s, w, bs, l, depth, new_s, old)
    keep = min(LEFT_CHUNKS * CHUNK, sp)
    return (xp.reshape(bp, sp, d), xs.reshape(bs, ts, d),
            new_p[0].reshape(depth, bp, sp, KV_LORA), new_p[1].reshape(depth, bp, sp, ROPE_DIM),
            new_p[2].reshape(depth, bp, keep, H_B, D_B), new_p[3].reshape(depth, bp, keep, H_B, D_B),
            new_s[0].reshape(depth, bs, ts, KV_LORA), new_s[1].reshape(depth, bs, ts, ROPE_DIM),
            new_s[2].reshape(depth, bs, ts, H_B, D_B), new_s[3].reshape(depth, bs, ts, H_B, D_B))
```

```python
import functools

import jax
import jax.numpy as jnp
from jax import lax
from jax.experimental import pallas as pl
from jax.experimental.pallas import tpu as pltpu

D_MODEL = 1024
DEPTH = 4
CHUNK = 64
H_A = 8
Q_LORA = 768
KV_LORA = 256
NOPE_DIM = 64
ROPE_DIM = 32
V_DIM = 64
ROPE_BASE = 10000.0
MLA_SCALE = (NOPE_DIM + ROPE_DIM) ** -0.5
LOG2E = 1.4426950408889634
H_B = 8
D_B = 64
LEFT_CHUNKS = 8
MAX_REL = 128
BAND_SCALE = D_B ** -0.5
D_FF = 2816
ALPHA = (2 * DEPTH) ** 0.25
NORM_EPS = 1e-5
NEG_INF = -1e30

LANES = 128
SUBLANES = 8
MXU_WIDTH = 256
HEAD_PAD = LANES
BAND_GROUP = 4
MLA_QB = 512
ROW_TILE = 512
FFN_ROW_TILE = 1024
FFN_CHUNK_TILES = 4
VMEM_LIMIT = 56 * 1024 * 1024

BF16 = jnp.bfloat16
F32 = jnp.float32


def _dot(a, b):
    return jnp.dot(a, b, preferred_element_type=F32)


def _dot_nt(a, b):
    return lax.dot_general(a, b, (((1,), (1,)), ((), ())), preferred_element_type=F32)


def _layer_norm(y, g, b):
    mu = jnp.mean(y, axis=-1, keepdims=True)
    d = y - mu
    var = jnp.mean(d * d, axis=-1, keepdims=True)
    return d * lax.rsqrt(var + NORM_EPS) * g + b


def _rms_norm(y, g):
    return y * lax.rsqrt(jnp.mean(y * y, axis=-1, keepdims=True) + NORM_EPS) * g


def _sigmoid(a):
    return 1.0 / (1.0 + jnp.exp(-a))


def _layer_spec(a, layer):
    zeros = (0,) * (a.ndim - 1)
    return pl.BlockSpec((None,) + a.shape[1:], lambda *_: (layer,) + zeros, pipeline_mode=pl.Buffered(1))


def _params(*sem):
    return pltpu.CompilerParams(dimension_semantics=sem, vmem_limit_bytes=VMEM_LIMIT)


def _ffn_chunks():
    tiles = D_FF // MXU_WIDTH
    assert tiles * MXU_WIDTH == D_FF
    bounds = list(range(0, tiles, FFN_CHUNK_TILES)) + [tiles]
    return tuple((lo * MXU_WIDTH, (hi - lo) * MXU_WIDTH) for lo, hi in zip(bounds, bounds[1:]))


def _ffn_ln_kernel(x_ref, w1_ref, w2_ref, g_ref, b_ref, o_ref):
    x = x_ref[...]
    xb = x.astype(BF16)
    acc = None
    for c0, ck in _ffn_chunks():
        a = _dot(xb, w1_ref[:, c0:c0 + ck])
        g = _dot(xb, w1_ref[:, D_FF + c0:D_FF + c0 + ck])
        h = (a * _sigmoid(a) * g).astype(BF16)
        part = _dot(h, w2_ref[c0:c0 + ck, :])
        acc = part if acc is None else acc + part
    o_ref[...] = _layer_norm(ALPHA * x + 0.5 * acc, g_ref[...], b_ref[...])


def _ffn_ln(x, w1, w2, g, b, layer):
    n, d = x.shape
    tm = min(FFN_ROW_TILE, n)
    row = pl.BlockSpec((tm, d), lambda i: (i, 0))
    return pl.pallas_call(
        _ffn_ln_kernel,
        grid=(n // tm,),
        in_specs=[row] + [_layer_spec(a, layer) for a in (w1, w2, g, b)],
        out_specs=row,
        out_shape=jax.ShapeDtypeStruct((n, d), F32),
        compiler_params=_params("parallel"),
    )(x, w1, w2, g, b)


N_PROJ_IN = 14
N_PROJ_CACHE = 4


def _proj_kernel(*refs):
    (x_ref, cq_ref, sq_ref, ckr_ref, wq_ref, wckv_ref, wkr_ref, wband_ref, qg_ref, kvg_ref,
     wqa_ref, wqr_ref, wkc_ref, wuv_ref) = refs[:N_PROJ_IN]
    (qcat_ref, kcat_ref, v_ref, qb_ref, kb_ref, vb_ref,
     ckv_ref, kr_ref, kbt_ref, vbt_ref) = refs[len(refs) - 6 - N_PROJ_CACHE:]
    xb = x_ref[...].astype(BF16)
    qn = _rms_norm(_dot(xb, wq_ref[...]), qg_ref[...]).astype(BF16)
    qa = _dot(qn, wqa_ref[...])
    qr = _dot(qn, wqr_ref[...])
    cq = cq_ref[...]
    sq = sq_ref[...]
    per_vreg = LANES // ROPE_DIM
    for h in range(H_A):
        sl = slice(h * HEAD_PAD, (h + 1) * HEAD_PAD)
        src = qr[:, (h // per_vreg) * LANES:(h // per_vreg + 1) * LANES]
        shift = (NOPE_DIM - (h % per_vreg) * ROPE_DIM) % LANES
        qs = pltpu.roll(src, shift, 1) if shift else src
        qcat_ref[:, sl] = (qa[:, sl] * cq + qs * sq).astype(BF16)
    ckv = _rms_norm(_dot(xb, wckv_ref[...]), kvg_ref[...])
    ckv_ref[...] = ckv
    t = _dot(xb, wkr_ref[...]) * ckr_ref[...]
    kr = t + pltpu.roll(t, LANES - ROPE_DIM, 1)
    kr_ref[...] = kr[:, :ROPE_DIM]
    lane = lax.broadcasted_iota(jnp.int32, kr.shape, 1)
    on_rope = (lane >= NOPE_DIM) & (lane < NOPE_DIM + ROPE_DIM)
    kr_placed = jnp.where(on_rope, pltpu.roll(kr, NOPE_DIM, 1), 0.0)
    ckvb = ckv.astype(BF16)
    kc = _dot(ckvb, wkc_ref[...])
    for h in range(H_A):
        sl = slice(h * HEAD_PAD, (h + 1) * HEAD_PAD)
        kcat_ref[:, sl] = (kc[:, sl] + kr_placed).astype(BF16)
    v_ref[...] = _dot(ckvb, wuv_ref[...]).astype(BF16)
    hb = H_B * D_B
    band = _dot(xb, wband_ref[...])
    qb_ref[...] = (band[:, :hb] * (BAND_SCALE * LOG2E)).astype(BF16)
    kb = band[:, hb:2 * hb]
    vb = band[:, 2 * hb:]
    kb_ref[...] = kb.astype(BF16)
    vb_ref[...] = vb.astype(BF16)
    kbt_ref[...] = kb
    vbt_ref[...] = vb


def _proj(x, tabs, w, batch, layer, depth, caches):
    n, d = x.shape
    tm = min(ROW_TILE, n)
    ntiles = n // tm
    s = n // batch
    keep = min(LEFT_CHUNKS * CHUNK, s)
    cq, sq, ckr = tabs
    period = cq.shape[0] // tm
    if s <= tm:
        assert keep == s
        tail_blocks, tail_idx = ntiles, (lambda i: i)
    else:
        per_seq, ntail = s // tm, keep // tm
        assert per_seq * tm == s and ntail * tm == keep
        tail_blocks = batch * ntail
        tail_idx = lambda i: (i // per_seq) * ntail + jnp.maximum(i % per_seq - (per_seq - ntail), 0)
    row = lambda c: pl.BlockSpec((tm, c), lambda i: (i, 0))
    tab = lambda c: pl.BlockSpec((tm, c), lambda i: (i % period, 0))
    stacked = lambda c: pl.BlockSpec((tm, c), lambda i: (layer * ntiles + i, 0))
    tail = lambda c: pl.BlockSpec((tm, c), lambda i: (layer * tail_blocks + tail_idx(i), 0))
    hp = H_A * HEAD_PAD
    hb = H_B * D_B
    out_shape = [
        jax.ShapeDtypeStruct((n, hp), BF16),
        jax.ShapeDtypeStruct((n, hp), BF16),
        jax.ShapeDtypeStruct((n, H_A * V_DIM), BF16),
        jax.ShapeDtypeStruct((n, hb), BF16),
        jax.ShapeDtypeStruct((n, hb), BF16),
        jax.ShapeDtypeStruct((n, hb), BF16),
        jax.ShapeDtypeStruct((depth * n, KV_LORA), F32),
        jax.ShapeDtypeStruct((depth * n, ROPE_DIM), F32),
        jax.ShapeDtypeStruct((depth * tail_blocks * tm, hb), F32),
        jax.ShapeDtypeStruct((depth * tail_blocks * tm, hb), F32),
    ]
    out_specs = [row(hp), row(hp), row(H_A * V_DIM), row(hb), row(hb), row(hb),
                 stacked(KV_LORA), stacked(ROPE_DIM), tail(hb), tail(hb)]
    weights = (w["w_q"], w["w_ckv"], w["w_kr2"], w["w_band"], w["q_g"], w["kv_g"],
               w["w_qa"], w["w_qr"], w["w_kc"], w["w_uv"])
    in_specs = ([row(d), tab(HEAD_PAD), tab(HEAD_PAD), tab(LANES)]
                + [_layer_spec(a, layer) for a in weights])
    args = [x, cq, sq, ckr, *weights]
    assert len(args) == N_PROJ_IN
    if caches is None:
        caches = tuple(jnp.zeros(o.shape, o.dtype) for o in out_shape[6:])
    in_specs += [pl.BlockSpec(memory_space=pl.ANY)] * N_PROJ_CACHE
    args += list(caches)
    aliases = {N_PROJ_IN + k: 6 + k for k in range(N_PROJ_CACHE)}
    outs = pl.pallas_call(
        _proj_kernel,
        grid=(ntiles,),
        in_specs=in_specs,
        out_specs=out_specs,
        out_shape=out_shape,
        input_output_aliases=aliases,
        compiler_params=_params("arbitrary"),
    )(*args)
    return outs[:6], tuple(outs[6:])


def _mla_prompt_kernel(q_lo_ref, q_hi_ref, k_ref, v_ref, o_ref):
    qb = MLA_QB
    rc = lax.broadcasted_iota(jnp.int32, (qb, qb), 0) // CHUNK
    cc = lax.broadcasted_iota(jnp.int32, (qb, qb), 1) // CHUNK
    diag_mask = cc <= rc
    first_half = lax.broadcasted_iota(jnp.int32, (qb, 2 * V_DIM), 1) < V_DIM

    def scores(q_ref, h, kv0):
        hs = slice(h * HEAD_PAD, (h + 1) * HEAD_PAD)
        q = q_ref[:, hs]
        s_d = jnp.where(diag_mask, _dot_nt(q, k_ref[kv0:kv0 + qb, hs]), NEG_INF)
        s_f = _dot_nt(q, k_ref[0:kv0, hs]) if kv0 else None
        return s_d, s_f

    def attend(h, kv0, s_d, s_f):
        vs = slice((h // 2) * 2 * V_DIM, (h // 2 + 1) * 2 * V_DIM)
        m = jnp.max(s_d, axis=-1, keepdims=True)
        if s_f is not None:
            m = jnp.maximum(m, jnp.max(s_f, axis=-1, keepdims=True))
        p_d = jnp.exp2(s_d - m)
        l = jnp.sum(p_d, axis=-1, keepdims=True)
        o = _dot(p_d.astype(BF16), v_ref[kv0:kv0 + qb, vs])
        if s_f is not None:
            p_f = jnp.exp2(s_f - m)
            l = l + jnp.sum(p_f, axis=-1, keepdims=True)
            o = o + _dot(p_f.astype(BF16), v_ref[0:kv0, vs])
        return o / l

    def query_block(q_ref, n):
        kv0 = n * qb
        ahead = scores(q_ref, 0, kv0)
        outs = []
        for h in range(H_A):
            cur = ahead
            if h + 1 < H_A:
                ahead = scores(q_ref, h + 1, kv0)
            outs.append(attend(h, kv0, *cur))
            if h % 2:
                hp = h // 2
                o_ref[kv0:kv0 + qb, hp * 2 * V_DIM:(hp + 1) * 2 * V_DIM] = (
                    jnp.where(first_half, outs[h - 1], outs[h]).astype(BF16))

    def variant(j):
        query_block(q_lo_ref, j)
        query_block(q_hi_ref, nq - 1 - j)

    nq = k_ref.shape[0] // qb
    for j in range(nq // 2):
        pl.when(pl.program_id(1) == j)(functools.partial(variant, j))


def _mla_prompt(qcat, kcat, v, batch):
    n = qcat.shape[0]
    s = n // batch
    nq = s // MLA_QB
    assert nq % 2 == 0
    seq = lambda c: pl.BlockSpec((s, c), lambda b, j: (b, 0))
    q_lo = pl.BlockSpec((MLA_QB, qcat.shape[1]), lambda b, j: (b * nq + j, 0))
    q_hi = pl.BlockSpec((MLA_QB, qcat.shape[1]), lambda b, j: (b * nq + nq - 1 - j, 0))
    return pl.pallas_call(
        _mla_prompt_kernel,
        grid=(batch, nq // 2),
        in_specs=[q_lo, q_hi, seq(kcat.shape[1]), seq(v.shape[1])],
        out_specs=seq(v.shape[1]),
        out_shape=jax.ShapeDtypeStruct((n, v.shape[1]), BF16),
        compiler_params=_params("parallel", "arbitrary"),
    )(qcat, qcat, kcat, v)


def _mla_sample_kernel(q_ref, ckv_ref, kr_ref, cckv_ref, ckr_ref, wabs_ref, wrope_ref, wuv_ref, o_ref):
    t = q_ref.shape[0]
    q_abs = jnp.concatenate(
        [_dot(q_ref[:, h * HEAD_PAD:(h + 1) * HEAD_PAD], wabs_ref[h]) for h in range(H_A)], axis=0).astype(BF16)
    q_rope = jnp.concatenate(
        [_dot(q_ref[:, h * HEAD_PAD:(h + 1) * HEAD_PAD], wrope_ref[h]) for h in range(H_A)], axis=0).astype(BF16)
    c_old = cckv_ref[...].astype(BF16)
    r_old = ckr_ref[...].astype(BF16)
    c_new = ckv_ref[...].astype(BF16)
    r_new = kr_ref[...].astype(BF16)
    s_old = _dot_nt(q_abs, c_old) + _dot_nt(q_rope, r_old)
    s_new = _dot_nt(q_abs, c_new) + _dot_nt(q_rope, r_new)
    m = jnp.maximum(jnp.max(s_old, axis=-1, keepdims=True), jnp.max(s_new, axis=-1, keepdims=True))
    p_old = jnp.exp2(s_old - m)
    p_new = jnp.exp2(s_new - m)
    l = jnp.sum(p_old, axis=-1, keepdims=True) + jnp.sum(p_new, axis=-1, keepdims=True)
    o_lat = ((_dot(p_old.astype(BF16), c_old) + _dot(p_new.astype(BF16), c_new)) / l).astype(BF16)
    group = lax.broadcasted_iota(jnp.int32, (t, H_A * V_DIM), 1) // V_DIM
    out = jnp.zeros((t, H_A * V_DIM), F32)
    for h in range(H_A):
        out = jnp.where(group == h, _dot(o_lat[h * t:(h + 1) * t], wuv_ref[...]), out)
    o_ref[...] = out.astype(BF16)


def _mla_sample(qcat, ckv_all, kr_all, cache_ckv, cache_kr, w, batch, layer):
    n = qcat.shape[0]
    t = n // batch
    past = cache_ckv.shape[2]
    row = lambda c: pl.BlockSpec((t, c), lambda b: (b, 0))
    new = lambda c: pl.BlockSpec((t, c), lambda b: (layer * batch + b, 0))
    old = lambda c: pl.BlockSpec((None, None, past, c), lambda b: (layer, b, 0, 0))
    weights = (w["w_abs"], w["w_ropesel"], w["w_uv"])
    return pl.pallas_call(
        _mla_sample_kernel,
        grid=(batch,),
        in_specs=[row(qcat.shape[1]), new(KV_LORA), new(ROPE_DIM), old(KV_LORA), old(ROPE_DIM)]
                 + [_layer_spec(a, layer) for a in weights],
        out_specs=row(H_A * V_DIM),
        out_shape=jax.ShapeDtypeStruct((n, H_A * V_DIM), BF16),
        compiler_params=_params("parallel"),
    )(qcat, ckv_all, kr_all, cache_ckv, cache_kr, *weights)


def _band_dims(s_len):
    group = min(BAND_GROUP, s_len // CHUNK)
    window = -(-(group + LEFT_CHUNKS) * CHUNK // LANES) * LANES
    return group, window, window - group * CHUNK


def _band_kernel(*refs, hist_rows, group, window, front):
    if hist_rows:
        tab_ref, q_ref, k_ref, v_ref, ck_ref, cv_ref, o_ref, kpad, vpad, bias = refs
    else:
        tab_ref, q_ref, k_ref, v_ref, o_ref, kpad, vpad, bias = refs
    s_len = q_ref.shape[0]
    hb = H_B * D_B
    gq = group * CHUNK
    span = (LEFT_CHUNKS + 1) * CHUNK

    @pl.when(pl.program_id(0) == 0)
    def _():
        kpad[0:front - hist_rows, :] = jnp.zeros((front - hist_rows, hb), BF16)
        vpad[0:front - hist_rows, :] = jnp.zeros((front - hist_rows, hb), BF16)
        wide = window + LANES
        j = lax.broadcasted_iota(jnp.int32, (SUBLANES, wide), 1)
        idx = jnp.clip(LEFT_CHUNKS * CHUNK + (CHUNK - 1) - j, -MAX_REL, MAX_REL) + MAX_REL
        u = lax.broadcasted_iota(jnp.int32, (CHUNK, window), 1)
        for h in range(H_B):
            def fill(d, b, h=h):
                return jnp.where(idx == d, tab_ref[h, d] * LOG2E, b)
            f = lax.fori_loop(0, 2 * MAX_REL + 1, fill, jnp.zeros((SUBLANES, wide), F32))
            skew = pltpu.roll(jnp.tile(f, (CHUNK // SUBLANES, 1)), wide - (CHUNK - 1), 1, stride=1, stride_axis=0)
            base = jnp.where(u < span, skew[:, :window], NEG_INF)
            for a in range(group):
                off = front + (a - LEFT_CHUNKS) * CHUNK
                r = (h % 2) * gq + a * CHUNK
                bias[h // 2, r:r + CHUNK, :] = pltpu.roll(base, off, 1) if off else base

    if hist_rows:
        kpad[front - hist_rows:front, :] = ck_ref[...].astype(BF16)
        vpad[front - hist_rows:front, :] = cv_ref[...].astype(BF16)
    kpad[front:front + s_len, :] = k_ref[...]
    vpad[front:front + s_len, :] = v_ref[...]

    slot = lax.broadcasted_iota(jnp.int32, (1, window), 1)
    first_half = lax.broadcasted_iota(jnp.int32, (gq, 2 * D_B), 1) < D_B

    def query_group(g, _, check_exists):
        r0 = pl.multiple_of(g * gq, gq)
        q = q_ref[pl.ds(r0, gq), :]
        kw = kpad[pl.ds(r0, window), :]
        vw = vpad[pl.ds(r0, window), :]
        exists = slot >= front - hist_rows - r0
        def scores(hp):
            sl = slice(hp * 2 * D_B, (hp + 1) * 2 * D_B)
            qp = q[:, sl]
            zero = jnp.zeros_like(qp)
            q2 = jnp.concatenate([jnp.where(first_half, qp, zero), jnp.where(first_half, zero, qp)], axis=0)
            s = _dot_nt(q2, kw[:, sl]) + bias[hp]
            return jnp.where(exists, s, NEG_INF) if check_exists else s

        ahead = scores(0)
        for hp in range(H_B // 2):
            sl = slice(hp * 2 * D_B, (hp + 1) * 2 * D_B)
            s = ahead
            if hp + 1 < H_B // 2:
                ahead = scores(hp + 1)
            p = jnp.exp2(s - jnp.max(s, axis=-1, keepdims=True))
            l = jnp.sum(p, axis=-1, keepdims=True)
            o = _dot(p.astype(BF16), vw[:, sl]) / l
            o_ref[pl.ds(r0, gq), sl] = jnp.where(first_half, o[:gq], o[gq:]).astype(BF16)
        return 0

    n_groups = s_len // gq
    n_checked = min(n_groups, -(-(front - hist_rows) // gq))
    lax.fori_loop(0, n_checked, functools.partial(query_group, check_exists=True), 0)
    lax.fori_loop(n_checked, n_groups, functools.partial(query_group, check_exists=False), 0)


def _band(table, qb, kb, vb, batch, layer, cache_k=None, cache_v=None):
    n, hb = qb.shape
    s = n // batch
    hist_rows = 0 if cache_k is None else cache_k.shape[2]
    group, window, front = _band_dims(s)
    row = pl.BlockSpec((s, hb), lambda b: (b, 0))
    in_specs = [pl.BlockSpec(memory_space=pltpu.SMEM), row, row, row]
    args = [table, qb, kb, vb]
    if hist_rows:
        cache = pl.BlockSpec((None, None, hist_rows, hb), lambda b: (layer, b, 0, 0))
        in_specs += [cache, cache]
        args += [cache_k, cache_v]
    return pl.pallas_call(
        functools.partial(_band_kernel, hist_rows=hist_rows, group=group, window=window, front=front),
        grid=(batch,),
        in_specs=in_specs,
        out_specs=row,
        out_shape=jax.ShapeDtypeStruct((n, hb), BF16),
        scratch_shapes=[pltpu.VMEM((front + s, hb), BF16), pltpu.VMEM((front + s, hb), BF16),
                        pltpu.VMEM((H_B // 2, 2 * group * CHUNK, window), F32)],
        compiler_params=_params("arbitrary"),
    )(*args)


def _merge_kernel(x_ref, oa_ref, ob_ref, wga_ref, wgb_ref, wpa_ref, wpb_ref, wout_ref, g_ref, b_ref, o_ref):
    half = x_ref.shape[0] // 2
    for r in range(2):
        rows = slice(r * half, (r + 1) * half)
        x = x_ref[rows, :]
        xb = x.astype(BF16)
        mix = (_sigmoid(_dot(xb, wga_ref[...])) * _dot(oa_ref[rows, :], wpa_ref[...])
               + _sigmoid(_dot(xb, wgb_ref[...])) * _dot(ob_ref[rows, :], wpb_ref[...]))
        y = ALPHA * x + _dot(mix.astype(BF16), wout_ref[...])
        o_ref[rows, :] = _layer_norm(y, g_ref[...], b_ref[...])


def _merge(x, oa, ob, w, layer):
    n, d = x.shape
    tm = min(ROW_TILE, n)
    row = lambda c: pl.BlockSpec((tm, c), lambda i: (i, 0))
    weights = (w["w_ga"], w["w_gb"], w["w_pa"], w["w_pb"], w["w_out"], w["ln2_g"], w["ln2_b"])
    return pl.pallas_call(
        _merge_kernel,
        grid=(n // tm,),
        in_specs=[row(d), row(oa.shape[1]), row(ob.shape[1])] + [_layer_spec(a, layer) for a in weights],
        out_specs=row(d),
        out_shape=jax.ShapeDtypeStruct((n, d), F32),
        compiler_params=_params("parallel"),
    )(x, oa, ob, *weights)


def _rope_tables(pos, rows):
    half = ROPE_DIM // 2
    inv = ROPE_BASE ** (-jnp.arange(half, dtype=F32) / half)
    ang = pos.astype(F32)[:, None] * inv[None, :]
    cos, sin = jnp.cos(ang), jnp.sin(ang)
    ck = jnp.concatenate([cos, cos], axis=-1)
    sk = jnp.concatenate([-sin, sin], axis=-1)
    t = pos.shape[0]
    pad = jnp.zeros((t, HEAD_PAD - NOPE_DIM - ROPE_DIM), F32)
    cq = jnp.concatenate([jnp.ones((t, NOPE_DIM), F32), ck, pad], axis=-1) * (MLA_SCALE * LOG2E)
    sq = jnp.concatenate([jnp.zeros((t, NOPE_DIM), F32), sk, pad], axis=-1) * (MLA_SCALE * LOG2E)
    ckr = jnp.concatenate([ck, sk, jnp.zeros((t, LANES - 2 * ROPE_DIM), F32)], axis=-1)
    rep = rows // t
    return tuple(jnp.tile(a, (rep, 1)) for a in (cq, sq, ckr))


def _swap_halves(w):
    half = w.shape[-1] // 2
    return jnp.concatenate([w[..., half:], w[..., :half]], axis=-1)


def _prep_weights(ln1_g, ln1_b, ffn1_w1, ffn1_w2, w_in, q_g, w_uq, kv_g, w_uk, w_uv, rel_bias,
                  w_pa, w_pb, w_out, ln2_g, ln2_b, ffn2_w1, ffn2_w2, ln3_g, ln3_b):
    depth = w_in.shape[0]
    hb = H_B * D_B
    c0, c1, c2 = Q_LORA, Q_LORA + KV_LORA, Q_LORA + KV_LORA + ROPE_DIM
    c3 = c2 + 3 * hb
    w_kr = w_in[:, :, c1:c2]
    qd = NOPE_DIM + ROPE_DIM
    uq = w_uq.reshape(depth, Q_LORA, H_A, qd)
    zq = jnp.zeros((depth, Q_LORA, H_A, HEAD_PAD - qd), F32)
    w_qa = jnp.concatenate([uq, zq], axis=-1).reshape(depth, Q_LORA, H_A * HEAD_PAD)
    w_qr = _swap_halves(uq[..., NOPE_DIM:]).reshape(depth, Q_LORA, H_A * ROPE_DIM)
    uk = w_uk.reshape(depth, KV_LORA, H_A, NOPE_DIM)
    w_kc = jnp.concatenate([uk, jnp.zeros((depth, KV_LORA, H_A, HEAD_PAD - NOPE_DIM), F32)],
                           axis=-1).reshape(depth, KV_LORA, H_A * HEAD_PAD)
    eye = jnp.eye(ROPE_DIM, dtype=F32)
    place = jnp.concatenate([jnp.zeros((ROPE_DIM, NOPE_DIM), F32), eye,
                             jnp.zeros((ROPE_DIM, HEAD_PAD - qd), F32)], axis=-1)
    w_abs = jnp.concatenate([jnp.transpose(uk, (0, 2, 3, 1)),
                             jnp.zeros((depth, H_A, HEAD_PAD - NOPE_DIM, KV_LORA), F32)], axis=2)
    w_ropesel = jnp.broadcast_to(place.T, (depth, H_A, HEAD_PAD, ROPE_DIM))
    bf = lambda a: a.astype(BF16)
    row = lambda a: a.reshape(depth, 1, -1)
    return dict(
        ln1_g=row(ln1_g), ln1_b=row(ln1_b), f1_w1=bf(ffn1_w1), f1_w2=bf(ffn1_w2),
        w_q=bf(w_in[:, :, :c0]), w_ckv=bf(w_in[:, :, c0:c1]),
        w_kr2=bf(jnp.concatenate([w_kr, _swap_halves(w_kr),
                                  jnp.zeros((depth, D_MODEL, LANES - 2 * ROPE_DIM), F32)], axis=-1)),
        w_band=bf(w_in[:, :, c2:c3]), w_ga=bf(w_in[:, :, c3:c3 + D_MODEL]), w_gb=bf(w_in[:, :, c3 + D_MODEL:]),
        q_g=row(q_g), kv_g=row(kv_g), w_qa=bf(w_qa), w_qr=bf(w_qr), w_kc=bf(w_kc),
        w_uv=bf(w_uv), w_abs=bf(w_abs), w_ropesel=bf(w_ropesel), rel_bias=rel_bias,
        w_pa=bf(w_pa), w_pb=bf(w_pb), w_out=bf(w_out), ln2_g=row(ln2_g), ln2_b=row(ln2_b),
        f2_w1=bf(ffn2_w1), f2_w2=bf(ffn2_w2), ln3_g=row(ln3_g), ln3_b=row(ln3_b),
    )


def _layer(x, tabs, w, batch, layer, depth, new_caches, old_caches=None):
    x = _ffn_ln(x, w["f1_w1"], w["f1_w2"], w["ln1_g"], w["ln1_b"], layer)
    (qcat, kcat, v, qb, kb, vb), new_caches = _proj(x, tabs, w, batch, layer, depth, new_caches)
    table = w["rel_bias"][layer]
    if old_caches is None:
        oa = _mla_prompt(qcat, kcat, v, batch)
        ob = _band(table, qb, kb, vb, batch, layer)
    else:
        c_ckv, c_kr, c_k, c_v = old_caches
        oa = _mla_sample(qcat, new_caches[0], new_caches[1], c_ckv, c_kr, w, batch, layer)
        ob = _band(table, qb, kb, vb, batch, layer, c_k, c_v)
    x = _merge(x, oa, ob, w, layer)
    x = _ffn_ln(x, w["f2_w1"], w["f2_w2"], w["ln3_g"], w["ln3_b"], layer)
    return x, new_caches


def kernel(x_prompt, x_sample, cache_mla_ckv, cache_mla_krope, cache_band_k, cache_band_v, ln1_g, ln1_b, ffn1_w1, ffn1_w2, w_in, mla_q_norm_g, mla_w_uq, mla_kv_norm_g, mla_w_uk, mla_w_uv, band_rel_bias, w_proj_a, w_proj_b, w_out, ln2_g, ln2_b, ffn2_w1, ffn2_w2, ln3_g, ln3_b):
    bp, sp, d = x_prompt.shape
    bs, ts, _ = x_sample.shape
    depth = ln1_g.shape[0]
    past = cache_mla_ckv.shape[2]
    hist = cache_band_k.shape[2]
    hb = H_B * D_B
    assert d == D_MODEL and sp % MLA_QB == 0 and ts == CHUNK and hist == LEFT_CHUNKS * CHUNK
    np_, ns = bp * sp, bs * ts
    tabs_p = _rope_tables(jnp.arange(sp, dtype=jnp.int32), max(sp, min(ROW_TILE, np_)))
    tabs_s = _rope_tables(past + jnp.arange(ts, dtype=jnp.int32), min(ROW_TILE, ns))
    w = _prep_weights(ln1_g, ln1_b, ffn1_w1, ffn1_w2, w_in, mla_q_norm_g, mla_w_uq, mla_kv_norm_g,
                      mla_w_uk, mla_w_uv, band_rel_bias, w_proj_a, w_proj_b, w_out, ln2_g, ln2_b,
                      ffn2_w1, ffn2_w2, ln3_g, ln3_b)
    old = (cache_mla_ckv, cache_mla_krope,
           cache_band_k.reshape(depth, bs, hist, hb), cache_band_v.reshape(depth, bs, hist, hb))
    xp = x_prompt.reshape(np_, d)
    xs = x_sample.reshape(ns, d)
    new_p = new_s = None
    for l in range(depth):
        xp, new_p = _layer(xp, tabs_p, w, bp, l, depth, new_p)
        xs, new_s = _layer(xs, tabs_s, w, bs, l, depth, new_s, old)
    keep = min(LEFT_CHUNKS * CHUNK, sp)
    return (xp.reshape(bp, sp, d), xs.reshape(bs, ts, d),
            new_p[0].reshape(depth, bp, sp, KV_LORA), new_p[1].reshape(depth, bp, sp, ROPE_DIM),
            new_p[2].reshape(depth, bp, keep, H_B, D_B), new_p[3].reshape(depth, bp, keep, H_B, D_B),
            new_s[0].reshape(depth, bs, ts, KV_LORA), new_s[1].reshape(depth, bs, ts, ROPE_DIM),
            new_s[2].reshape(depth, bs, ts, H_B, D_B), new_s[3].reshape(depth, bs, ts, H_B, D_B))
```
